```python
import jax, jax.numpy as jnp
from jax import lax
import numpy as np

D_MODEL = 4096
BATCH = 4
SEQ = 2048
DEPTH = 4
DEC_BATCH = 128
DEC_SEQ = 8
PAST_LEN = 16384
PAGE_SIZE = 128

W_LRU = D_MODEL // 4
LRU_BLOCKS = 8
LRU_BS = W_LRU // LRU_BLOCKS
LRU_CONV = 4
LRU_C = 8.0
W_HGRN = 3 * D_MODEL // 8
HGRN_KDIM = 128
HGRN_HEADS = W_HGRN // HGRN_KDIM
HGRN_VDIM = W_HGRN // HGRN_HEADS
HGRN_CHUNK = 16
W_RWKV = 3 * D_MODEL // 8
RWKV_HDIM = 64
RWKV_HEADS = W_RWKV // RWKV_HDIM
D_DECAY = max(32, int(round(1.8 * D_MODEL ** 0.5 / 32)) * 32)
D_AAA = max(32, int(round(1.8 * D_MODEL ** 0.5 / 32)) * 32)
D_GATE = max(32, int(round(0.6 * D_MODEL ** 0.8 / 32)) * 32)
RWKV_GN_EPS = 64e-5
W_MIX = W_LRU + W_HGRN + W_RWKV
N_LRU_COLS = 2 * W_LRU
N_HGRN_COLS = 4 * W_HGRN
N_RWKV_COLS = 3 * W_RWKV + D_DECAY + D_AAA + D_GATE
N_IN = N_LRU_COLS + N_HGRN_COLS + N_RWKV_COLS
FFN_DIM = ((8 * D_MODEL // 3 + 255) // 256) * 256
FFN_CONV = 3
PLE_DIM = 256
EPS = 1e-6

kernel_name = "hybrid_rglru_hgrn2_rwkv7_decode_step"

F32 = jnp.float32


def rms_norm(x, w):
    xf = x.astype(F32)
    y = xf * lax.rsqrt(jnp.mean(xf * xf, axis=-1, keepdims=True) + EPS)
    return (y * w.astype(F32)).astype(x.dtype)


def causal_depthwise_conv(u, buf, w, b):
    k = w.shape[0]
    t = u.shape[1]
    full = jnp.concatenate([buf.astype(u.dtype), u], axis=1)
    out = b.astype(u.dtype) + sum(full[:, j:j + t] * w[j].astype(u.dtype) for j in range(k))
    return out, full[:, t:]


def rg_lru(xa, h0, wr, br, wi, bi, lam):
    bsz, t, _ = xa.shape
    xf = xa.astype(F32)
    xb = xf.reshape(bsz, t, LRU_BLOCKS, LRU_BS)
    r = jax.nn.sigmoid(jnp.einsum('btnd,nde->btne', xb, wr.astype(F32)).reshape(bsz, t, W_LRU) + br.astype(F32))
    ig = jax.nn.sigmoid(jnp.einsum('btnd,nde->btne', xb, wi.astype(F32)).reshape(bsz, t, W_LRU) + bi.astype(F32))
    log_a = LRU_C * r * jax.nn.log_sigmoid(lam.astype(F32))
    a = jnp.exp(log_a)
    u = jnp.sqrt(-jnp.expm1(2.0 * log_a)) * ig * xf

    def step(h, au):
        a_t, u_t = au
        h = a_t * h + u_t
        return h, h

    h_last, hs = lax.scan(step, h0.astype(F32), (a.transpose(1, 0, 2), u.transpose(1, 0, 2)))
    return hs.transpose(1, 0, 2), h_last


def hgrn2_chunked(q, k, v, logf, s0):
    bsz, t = q.shape[:2]
    pad = (-t) % HGRN_CHUNK
    nc = (t + pad) // HGRN_CHUNK

    def to_chunks(z):
        z = jnp.pad(z, ((0, 0), (0, pad), (0, 0), (0, 0)))
        return z.reshape(bsz, nc, HGRN_CHUNK, z.shape[2], z.shape[3]).transpose(1, 0, 3, 2, 4)

    mask = jnp.tril(jnp.ones((HGRN_CHUNK, HGRN_CHUNK), bool))
    scale = HGRN_KDIM ** -0.5

    def step(s, inp):
        qc, kc, vc, gc = inp
        b = jnp.cumsum(gc, axis=2)
        inter = jnp.einsum('bhck,bhkv->bhcv', qc * jnp.exp(b), s)
        diff = jnp.where(mask[:, :, None], b[:, :, :, None, :] - b[:, :, None, :, :], -jnp.inf)
        att = jnp.einsum('bhtk,bhsk,bhtsk->bhts', qc, kc, jnp.exp(diff))
        o = (inter + jnp.einsum('bhts,bhsv->bhtv', att, vc)) * scale
        b_last = b[:, :, -1:, :]
        s = jnp.exp(b_last[:, :, 0, :])[..., None] * s + jnp.einsum('bhsk,bhsv->bhkv', kc * jnp.exp(b_last - b), vc)
        return s, o

    s_last, o = lax.scan(step, s0.astype(F32), (to_chunks(q), to_chunks(k), to_chunks(v), to_chunks(logf)))
    o = o.transpose(1, 0, 3, 2, 4).reshape(bsz, nc * HGRN_CHUNK, HGRN_HEADS, HGRN_VDIM)[:, :t]
    return o, s_last


def rwkv7_scan(r, w, k, v, kk, a, s0):
    def step(s, inp):
        r_t, w_t, k_t, v_t, kk_t, a_t = inp
        sa = jnp.einsum('bhvk,bhk->bhv', s, -kk_t)
        s = s * w_t[:, :, None, :] + sa[..., None] * (kk_t * a_t)[:, :, None, :] + v_t[..., None] * k_t[:, :, None, :]
        return s, jnp.einsum('bhvk,bhk->bhv', s, r_t)

    s_last, ys = lax.scan(step, s0.astype(F32), (r, w, k, v, kk, a))
    return ys, s_last


def layer(x, p_i, st, prm, lb):
    lru_h, lru_conv, hgrn_s, rwkv_s, rwkv_shift, ffn_conv = st
    bsz, t, _ = x.shape
    h = rms_norm(x, prm['attn_norm'])
    z = h @ prm['w_in']
    cuts = list(np.cumsum([W_LRU, W_LRU, W_HGRN, W_HGRN, W_HGRN, W_HGRN]))
    za, zg, zq, zf, zi, zgh, zc = jnp.split(z, cuts, axis=-1)

    xa, new_lru_conv = causal_depthwise_conv(za, lru_conv, prm['lru_conv_w'], prm['lru_conv_b'])
    ya, new_lru_h = rg_lru(xa, lru_h, prm['lru_wr'], prm['lru_br'], prm['lru_wi'], prm['lru_bi'], prm['lru_lambda'])
    ya = ya * jax.nn.gelu(zg.astype(F32), approximate=True)

    fgate = lb + (1.0 - lb) * jax.nn.sigmoid(zf.astype(F32))
    hshape = (bsz, t, HGRN_HEADS, HGRN_KDIM)
    qh = jax.nn.silu(zq.astype(F32)).reshape(hshape)
    kh = (1.0 - fgate).reshape(hshape)
    logf = jnp.log(fgate).reshape(hshape)
    vh = zi.astype(F32).reshape(bsz, t, HGRN_HEADS, HGRN_VDIM)
    ob, new_hgrn = hgrn2_chunked(qh, kh, vh, logf, hgrn_s)
    ob = ob * lax.rsqrt(jnp.mean(ob * ob, axis=-1, keepdims=True) + EPS) * prm['hgrn_norm'].astype(F32)
    yb = ob.reshape(bsz, t, W_HGRN) * jax.nn.silu(zgh.astype(F32))

    zcf = zc.astype(F32)
    prev = jnp.concatenate([rwkv_shift.astype(F32)[:, None], zcf[:, :-1]], axis=1)
    zmix = zcf + (prev - zcf) * prm['rwkv_mu'].astype(F32)
    new_shift = zc[:, -1]
    rcuts = list(np.cumsum([W_RWKV, D_DECAY, W_RWKV, W_RWKV, D_AAA]))
    rr, wd, kr, vr, ad, gd = jnp.split(zmix, rcuts, axis=-1)
    w_log = -jax.nn.softplus(-(prm['rwkv_w0'].astype(F32) + jnp.tanh(wd) @ prm['rwkv_w2'].astype(F32))) - 0.5
    decay = jnp.exp(-jnp.exp(w_log))
    aa = jax.nn.sigmoid(prm['rwkv_a0'].astype(F32) + ad @ prm['rwkv_a2'].astype(F32))
    gate_c = jax.nn.sigmoid(gd) @ prm['rwkv_g2'].astype(F32)
    rshape = (bsz, t, RWKV_HEADS, RWKV_HDIM)
    kk = (kr * prm['rwkv_k_k'].astype(F32)).reshape(rshape)
    kk = kk / jnp.maximum(jnp.sqrt(jnp.sum(kk * kk, axis=-1, keepdims=True)), 1e-12)
    kmod = kr * (1.0 + (aa - 1.0) * prm['rwkv_k_a'].astype(F32))
    rh, wh, kh2, vh2, ah = [u.reshape(rshape) for u in (rr, decay, kmod, vr, aa)]
    tb = lambda u: u.transpose(1, 0, 2, 3)
    yc, new_rwkv = rwkv7_scan(tb(rh), tb(wh), tb(kh2), tb(vh2), tb(kk), tb(ah), rwkv_s)
    yc = yc.transpose(1, 0, 2, 3)
    mu = jnp.mean(yc, axis=-1, keepdims=True)
    var = jnp.mean(jnp.square(yc - mu), axis=-1, keepdims=True)
    yc = ((yc - mu) * lax.rsqrt(var + RWKV_GN_EPS)).reshape(bsz, t, W_RWKV) * prm['rwkv_lnx_w'].astype(F32) + prm['rwkv_lnx_b'].astype(F32)
    bonus = jnp.sum(rh * kh2 * prm['rwkv_r_k'].astype(F32), axis=-1, keepdims=True) * vh2
    yc = (yc + bonus.reshape(bsz, t, W_RWKV)) * gate_c

    mix = jnp.concatenate([ya, yb, yc], axis=-1).astype(x.dtype)
    x = x + mix @ prm['w_out']

    h2 = rms_norm(x, prm['ffn_norm'])
    u = h2 @ prm['ffn_up']
    u, new_ffn_conv = causal_depthwise_conv(u, ffn_conv, prm['ffn_conv_w'], prm['ffn_conv_b'])
    ug, uv = jnp.split(u, 2, axis=-1)
    x = x + (jax.nn.gelu(ug.astype(F32), approximate=True) * uv.astype(F32)).astype(x.dtype) @ prm['ffn_down']

    g = jax.nn.sigmoid((rms_norm(x, prm['ple_norm']) @ prm['ple_gate']).astype(F32))
    x = x + ((p_i @ prm['ple_proj']).astype(F32) * g).astype(x.dtype)

    new = (new_lru_h, new_lru_conv, new_hgrn, new_rwkv, new_shift, new_ffn_conv)
    return x, tuple(n.astype(s.dtype) for n, s in zip(new, st))


def trunk(x, p, states, prm, lbs, final_norm):
    outs = [[] for _ in states]
    for i in range(DEPTH):
        x, ns = layer(x, p[i], tuple(s[i] for s in states), {k: v[i] for k, v in prm.items()}, lbs[i])
        for lst, n in zip(outs, ns):
            lst.append(n)
    return rms_norm(x, final_norm), tuple(jnp.stack(lst) for lst in outs)


def _normal(k, shape, scale):
    return scale * jax.random.normal(k, shape, F32)


def setup_inputs(seed: int = 0) -> dict:
    key = jax.random.key(seed)
    ks = jax.random.split(key, 48)
    L = DEPTH
    lam_u = jax.random.uniform(ks[20], (L, W_LRU), F32, 0.9, 0.999)
    lam_s = lam_u ** (1.0 / LRU_C)
    return {
        'x_prompt': _normal(ks[0], (BATCH, SEQ, D_MODEL), 1.0),
        'x_sample': _normal(ks[1], (DEC_BATCH, DEC_SEQ, D_MODEL), 1.0),
        'state_lru_h': _normal(ks[2], (L, DEC_BATCH, W_LRU), 0.5),
        'state_lru_conv': _normal(ks[3], (L, DEC_BATCH, LRU_CONV - 1, W_LRU), 1.0),
        'state_hgrn': _normal(ks[4], (L, DEC_BATCH, HGRN_HEADS, HGRN_KDIM, HGRN_VDIM), 0.3),
        'state_rwkv': _normal(ks[5], (L, DEC_BATCH, RWKV_HEADS, RWKV_HDIM, RWKV_HDIM), 0.3),
        'state_rwkv_shift': _normal(ks[6], (L, DEC_BATCH, N_RWKV_COLS), 1.0),
        'state_ffn_conv': _normal(ks[7], (L, DEC_BATCH, FFN_CONV - 1, 2 * FFN_DIM), 1.0),
        'p_prompt': _normal(ks[8], (L, BATCH, SEQ, PLE_DIM), 1.0),
        'p_sample': _normal(ks[9], (L, DEC_BATCH, DEC_SEQ, PLE_DIM), 1.0),
        'attn_norm': 1.0 + _normal(ks[10], (L, D_MODEL), 0.02),
        'w_in': _normal(ks[11], (L, D_MODEL, N_IN), D_MODEL ** -0.5),
        'lru_conv_w': _normal(ks[12], (L, LRU_CONV, W_LRU), LRU_CONV ** -0.5),
        'lru_conv_b': _normal(ks[13], (L, W_LRU), 0.02),
        'lru_wr': _normal(ks[14], (L, LRU_BLOCKS, LRU_BS, LRU_BS), LRU_BS ** -0.5),
        'lru_br': _normal(ks[15], (L, W_LRU), 0.02),
        'lru_wi': _normal(ks[16], (L, LRU_BLOCKS, LRU_BS, LRU_BS), LRU_BS ** -0.5),
        'lru_bi': _normal(ks[17], (L, W_LRU), 0.02),
        'lru_lambda': jnp.log(lam_s) - jnp.log1p(-lam_s),
        'hgrn_lb': _normal(ks[18], (L, W_HGRN), 0.5),
        'hgrn_norm': 1.0 + _normal(ks[19], (L, HGRN_VDIM), 0.02),
        'rwkv_mu': jax.random.uniform(ks[21], (L, N_RWKV_COLS), F32, 0.0, 1.0),
        'rwkv_w0': -1.0 + _normal(ks[22], (L, W_RWKV), 0.5),
        'rwkv_w2': _normal(ks[23], (L, D_DECAY, W_RWKV), 0.1 * D_DECAY ** -0.5),
        'rwkv_a0': _normal(ks[24], (L, W_RWKV), 0.5),
        'rwkv_a2': _normal(ks[25], (L, D_AAA, W_RWKV), 0.1 * D_AAA ** -0.5),
        'rwkv_g2': _normal(ks[26], (L, D_GATE, W_RWKV), D_GATE ** -0.5),
        'rwkv_k_k': 0.85 + _normal(ks[27], (L, W_RWKV), 0.05),
        'rwkv_k_a': 1.0 + _normal(ks[28], (L, W_RWKV), 0.05),
        'rwkv_r_k': _normal(ks[29], (L, RWKV_HEADS, RWKV_HDIM), 0.1),
        'rwkv_lnx_w': 1.0 + _normal(ks[30], (L, W_RWKV), 0.02),
        'rwkv_lnx_b': _normal(ks[31], (L, W_RWKV), 0.02),
        'w_out': _normal(ks[32], (L, W_MIX, D_MODEL), 0.5 * W_MIX ** -0.5),
        'ffn_norm': 1.0 + _normal(ks[33], (L, D_MODEL), 0.02),
        'ffn_up': _normal(ks[34], (L, D_MODEL, 2 * FFN_DIM), D_MODEL ** -0.5),
        'ffn_conv_w': _normal(ks[35], (L, FFN_CONV, 2 * FFN_DIM), FFN_CONV ** -0.5),
        'ffn_conv_b': _normal(ks[36], (L, 2 * FFN_DIM), 0.02),
        'ffn_down': _normal(ks[37], (L, FFN_DIM, D_MODEL), 0.5 * FFN_DIM ** -0.5),
        'ple_norm': 1.0 + _normal(ks[38], (L, D_MODEL), 0.02),
        'ple_gate': _normal(ks[39], (L, D_MODEL, D_MODEL), D_MODEL ** -0.5),
        'ple_proj': _normal(ks[40], (L, PLE_DIM, D_MODEL), 0.5 * PLE_DIM ** -0.5),
        'final_norm': 1.0 + _normal(ks[41], (D_MODEL,), 0.02),
    }


def reference(x_prompt, x_sample, state_lru_h, state_lru_conv, state_hgrn, state_rwkv, state_rwkv_shift,
              state_ffn_conv, p_prompt, p_sample, attn_norm, w_in, lru_conv_w, lru_conv_b, lru_wr, lru_br,
              lru_wi, lru_bi, lru_lambda, hgrn_lb, hgrn_norm, rwkv_mu, rwkv_w0, rwkv_w2, rwkv_a0, rwkv_a2,
              rwkv_g2, rwkv_k_k, rwkv_k_a, rwkv_r_k, rwkv_lnx_w, rwkv_lnx_b, w_out, ffn_norm, ffn_up,
              ffn_conv_w, ffn_conv_b, ffn_down, ple_norm, ple_gate, ple_proj, final_norm):
    prm = dict(attn_norm=attn_norm, w_in=w_in, lru_conv_w=lru_conv_w, lru_conv_b=lru_conv_b, lru_wr=lru_wr,
               lru_br=lru_br, lru_wi=lru_wi, lru_bi=lru_bi, lru_lambda=lru_lambda, hgrn_norm=hgrn_norm,
               rwkv_mu=rwkv_mu, rwkv_w0=rwkv_w0, rwkv_w2=rwkv_w2, rwkv_a0=rwkv_a0, rwkv_a2=rwkv_a2,
               rwkv_g2=rwkv_g2, rwkv_k_k=rwkv_k_k, rwkv_k_a=rwkv_k_a, rwkv_r_k=rwkv_r_k,
               rwkv_lnx_w=rwkv_lnx_w, rwkv_lnx_b=rwkv_lnx_b, w_out=w_out, ffn_norm=ffn_norm, ffn_up=ffn_up,
               ffn_conv_w=ffn_conv_w, ffn_conv_b=ffn_conv_b, ffn_down=ffn_down, ple_norm=ple_norm,
               ple_gate=ple_gate, ple_proj=ple_proj)
    lbs = jnp.cumsum(jax.nn.softmax(hgrn_lb.astype(F32), axis=0), axis=0)
    lbs = lbs - lbs[0]

    sample_states = (state_lru_h, state_lru_conv, state_hgrn, state_rwkv, state_rwkv_shift, state_ffn_conv)
    bp = x_prompt.shape[0]
    prompt_states = tuple(jnp.zeros((DEPTH, bp) + s.shape[2:], x_prompt.dtype) for s in sample_states)

    y_prompt, st_p = trunk(x_prompt, p_prompt, prompt_states, prm, lbs, final_norm)
    y_sample, st_s = trunk(x_sample, p_sample, sample_states, prm, lbs, final_norm)
    lru_h_p, lru_conv_p, hgrn_p, rwkv_p, shift_p, ffn_conv_p = st_p
    lru_h_s, lru_conv_s, hgrn_s, rwkv_s, shift_s, ffn_conv_s = st_s
    return (y_prompt, y_sample, lru_h_p, lru_conv_p, hgrn_p, rwkv_p, shift_p, ffn_conv_p,
            lru_h_s, lru_conv_s, hgrn_s, rwkv_s, shift_s, ffn_conv_s)
```

```python
import functools

import jax
import jax.numpy as jnp
import numpy as np
from jax import lax
from jax.experimental import pallas as pl
from jax.experimental.pallas import tpu as pltpu

F32 = jnp.float32
BF16 = jnp.bfloat16

D_MODEL = 4096
DEPTH = 4
W_LRU = 1024
LRU_BLOCKS = 8
LRU_BS = 128
LRU_CONV = 4
LRU_C = 8.0
W_HGRN = 1536
HGRN_KDIM = 128
HGRN_HEADS = 12
HGRN_VDIM = 128
HGRN_CHUNK = 16
W_RWKV = 1536
RWKV_HDIM = 64
RWKV_HEADS = 24
RWKV_PAIRS = RWKV_HEADS // 2
D_DECAY = 128
D_AAA = 128
D_GATE = 480
D_GATE_PAD = 512
RWKV_GN_EPS = 64e-5
N_RWKV_COLS = 3 * W_RWKV + D_DECAY + D_AAA + D_GATE
N_IN = 2 * W_LRU + 4 * W_HGRN + N_RWKV_COLS
FFN_DIM = 11008
FFN_CONV = 3
PLE_DIM = 256
EPS = 1e-6

LANES = 128
SUBLANES = 8
VMEM_LIMIT_BYTES = 56 * 1024 * 1024

ZC_OFF = 2 * W_LRU + 4 * W_HGRN
Z_WD = ZC_OFF
Z_AD = Z_WD + D_DECAY
Z_R = Z_AD + D_AAA
Z_K = Z_R + W_RWKV
Z_V = Z_K + W_RWKV
Z_GD = 13312
N_IN_PAD = Z_GD + D_GATE_PAD
ZC_REF_ORDER = ((Z_R, W_RWKV), (Z_WD, D_DECAY), (Z_K, W_RWKV), (Z_V, W_RWKV), (Z_AD, D_AAA), (Z_GD, D_GATE))


def _zc_to_kernel_layout(a):
    out = jnp.zeros(a.shape[:-1] + (N_IN_PAD,), a.dtype)
    pos = 0
    for off, width in ZC_REF_ORDER:
        out = lax.dynamic_update_slice_in_dim(out, a[..., pos:pos + width], off, axis=a.ndim - 1)
        pos += width
    return out


def _zc_from_kernel_layout(z):
    return jnp.concatenate([z[..., off:off + width] for off, width in ZC_REF_ORDER], axis=-1)


def _params(sem):
    return pltpu.CompilerParams(dimension_semantics=sem, vmem_limit_bytes=VMEM_LIMIT_BYTES)


def _sigmoid(x):
    return 1.0 / (1.0 + jnp.exp(-x))


def _gelu_tanh(x):
    return 0.5 * x * (1.0 + jnp.tanh(0.7978845608028654 * (x + 0.044715 * x * x * x)))


def _softplus(x):
    return jnp.maximum(x, 0.0) + jnp.log1p(jnp.exp(-jnp.abs(x)))


def _rmsnorm_kernel(x_ref, w_ref, o_ref):
    x = x_ref[...]
    ms = jnp.mean(x * x, axis=-1, keepdims=True)
    o_ref[...] = (x * lax.rsqrt(ms + EPS) * w_ref[...]).astype(o_ref.dtype)


def rmsnorm(x, w, out_dtype, tm=256):
    n, d = x.shape
    tm = min(tm, n)
    return pl.pallas_call(
        _rmsnorm_kernel,
        grid=(n // tm,),
        in_specs=[pl.BlockSpec((tm, d), lambda i: (i, 0)), pl.BlockSpec((1, d), lambda i: (0, 0))],
        out_specs=pl.BlockSpec((tm, d), lambda i: (i, 0)),
        out_shape=jax.ShapeDtypeStruct((n, d), out_dtype),
        compiler_params=_params(("parallel",)),
        name="rmsnorm",
    )(x, w.reshape(1, d))


def _mm_kernel(a_ref, w_ref, o_ref):
    o_ref[...] = jnp.dot(a_ref[...], w_ref[...], preferred_element_type=F32)


def _mm_res_kernel(a_ref, w_ref, r_ref, o_ref):
    o_ref[...] = r_ref[...] + jnp.dot(a_ref[...], w_ref[...], preferred_element_type=F32)


def matmul(a, w, res=None, *, tm, tn, name):
    m, k = a.shape
    n = w.shape[1]
    tm = min(tm, m)
    in_specs = [pl.BlockSpec((tm, k), lambda i, j: (i, 0)), pl.BlockSpec((k, tn), lambda i, j: (0, j))]
    args = [a, w]
    kern = _mm_kernel
    if res is not None:
        in_specs.append(pl.BlockSpec((tm, tn), lambda i, j: (i, j)))
        args.append(res)
        kern = _mm_res_kernel
    return pl.pallas_call(
        kern,
        grid=(m // tm, n // tn),
        in_specs=in_specs,
        out_specs=pl.BlockSpec((tm, tn), lambda i, j: (i, j)),
        out_shape=jax.ShapeDtypeStruct((m, n), F32),
        compiler_params=_params(("parallel", "parallel")),
        name=name,
    )(*args)


def _outproj_kernel(a1_ref, a2_ref, a3_ref, w1_ref, w2_ref, w3_ref, r_ref, o_ref):
    acc = jnp.dot(a1_ref[...], w1_ref[...], preferred_element_type=F32)
    acc += jnp.dot(a2_ref[...], w2_ref[...], preferred_element_type=F32)
    acc += jnp.dot(a3_ref[...], w3_ref[...], preferred_element_type=F32)
    o_ref[...] = r_ref[...] + acc


def outproj(ya, yb, yc, w1, w2, w3, res, *, tm=1024, tn=512):
    m = ya.shape[0]
    n = w1.shape[1]
    tm = min(tm, m)
    a_specs = [pl.BlockSpec((tm, a.shape[1]), lambda i, j: (i, 0)) for a in (ya, yb, yc)]
    w_specs = [pl.BlockSpec((w.shape[0], tn), lambda i, j: (0, j)) for w in (w1, w2, w3)]
    return pl.pallas_call(
        _outproj_kernel,
        grid=(m // tm, n // tn),
        in_specs=a_specs + w_specs + [pl.BlockSpec((tm, tn), lambda i, j: (i, j))],
        out_specs=pl.BlockSpec((tm, tn), lambda i, j: (i, j)),
        out_shape=jax.ShapeDtypeStruct((m, n), F32),
        compiler_params=_params(("parallel", "parallel")),
        name="outproj",
    )(ya, yb, yc, w1, w2, w3, res)


def _ple_kernel(h_ref, wg_ref, p_ref, wp_ref, x_ref, o_ref):
    g = _sigmoid(jnp.dot(h_ref[...], wg_ref[...], preferred_element_type=F32))
    pp = jnp.dot(p_ref[...], wp_ref[...], preferred_element_type=F32)
    o_ref[...] = x_ref[...] + pp * g


def ple(h, wg, p, wp, x, *, tm=1024, tn=512):
    m, d = x.shape
    tm = min(tm, m)
    return pl.pallas_call(
        _ple_kernel,
        grid=(m // tm, d // tn),
        in_specs=[pl.BlockSpec((tm, h.shape[1]), lambda i, j: (i, 0)),
                  pl.BlockSpec((wg.shape[0], tn), lambda i, j: (0, j)),
                  pl.BlockSpec((tm, p.shape[1]), lambda i, j: (i, 0)),
                  pl.BlockSpec((wp.shape[0], tn), lambda i, j: (0, j)),
                  pl.BlockSpec((tm, tn), lambda i, j: (i, j))],
        out_specs=pl.BlockSpec((tm, tn), lambda i, j: (i, j)),
        out_shape=jax.ShapeDtypeStruct((m, d), F32),
        compiler_params=_params(("parallel", "parallel")),
        name="ple",
    )(h, wg, p, wp, x)


def _lbs_kernel(lb_ref, o_ref):
    x = lb_ref[...]
    mx = jnp.max(x, axis=0, keepdims=True)
    e = jnp.exp(x - mx)
    sm = e / jnp.sum(e, axis=0, keepdims=True)
    acc = jnp.zeros_like(sm[0:1])
    for l in range(1, DEPTH):
        acc = acc + sm[l:l + 1]
        o_ref[l:l + 1, :] = acc
    o_ref[0:1, :] = jnp.zeros_like(acc)


def hgrn_lower_bounds(hgrn_lb):
    return pl.pallas_call(
        _lbs_kernel,
        out_shape=jax.ShapeDtypeStruct(hgrn_lb.shape, F32),
        name="hgrn_lbs",
    )(hgrn_lb)


def _shift_rows(x, hist, j, t8):
    r = x.shape[0]
    return jnp.where(t8 >= j, pltpu.roll(x, j, 0), pltpu.roll(hist, r - SUBLANES + j, 0))


def _hist_tile(x, hist8):
    r = x.shape[0]
    if r == SUBLANES:
        return hist8
    return jnp.concatenate([hist8, x[:r - SUBLANES]], axis=0)


def _lru_kernel(za_ref, zg_ref, hist_ref, h0_ref, cw_ref, cb_ref, wr_ref, br_ref, wi_ref, bi_ref,
                lam_ref, ya_ref, hout_ref, prev8_scr, hcar_scr, *, sb_n, r_n):
    i = pl.program_id(1)
    first = i == 0
    t8 = lax.broadcasted_iota(jnp.int32, (r_n, 1), 0) % SUBLANES
    lam = lam_ref[...]
    log_sig_lam = -_softplus(-lam)
    cw = cw_ref[...]
    for sb in range(sb_n):
        x = za_ref[sb]
        hist8 = jnp.where(first, hist_ref[sb], prev8_scr[...])
        hist = _hist_tile(x, hist8)
        xa = cb_ref[...] + cw[LRU_CONV - 1:LRU_CONV] * x
        for j in range(1, LRU_CONV):
            xa = xa + cw[LRU_CONV - 1 - j:LRU_CONV - j] * _shift_rows(x, hist, j, t8)
        prev8_scr[...] = x[r_n - SUBLANES:]
        r_lin, i_lin = [], []
        for n in range(LRU_BLOCKS):
            xb = xa[:, n * LRU_BS:(n + 1) * LRU_BS].astype(BF16)
            r_lin.append(jnp.dot(xb, wr_ref[n], preferred_element_type=F32))
            i_lin.append(jnp.dot(xb, wi_ref[n], preferred_element_type=F32))
        rg = _sigmoid(jnp.concatenate(r_lin, axis=1) + br_ref[...])
        ig = _sigmoid(jnp.concatenate(i_lin, axis=1) + bi_ref[...])
        log_a = LRU_C * rg * log_sig_lam
        a = jnp.exp(log_a)
        th = jnp.tanh(log_a)
        u = jnp.sqrt(-2.0 * th / (1.0 - th)) * ig * xa
        for s in (1, 2, 4):
            m = t8 >= s
            a_s = pltpu.roll(a, s, 0)
            u_s = pltpu.roll(u, s, 0)
            u = jnp.where(m, a * u_s + u, u)
            a = jnp.where(m, a * a_s, a)
        hprev = jnp.where(first, h0_ref[sb], hcar_scr[...])
        hs = []
        for g in range(r_n // SUBLANES):
            h = a[g * SUBLANES:(g + 1) * SUBLANES] * hprev + u[g * SUBLANES:(g + 1) * SUBLANES]
            hs.append(h)
            hprev = h[SUBLANES - 1:SUBLANES]
        hall = hs[0] if len(hs) == 1 else jnp.concatenate(hs, axis=0)
        hcar_scr[...] = hprev
        hout_ref[sb] = hprev
        ya_ref[sb] = (hall * _gelu_tanh(zg_ref[sb])).astype(ya_ref.dtype)


def lru_mixer(z3, hist, h0, prm, *, sb_n, r_n):
    nseq, seq, _ = z3.shape
    assert sb_n == 1 or r_n == seq
    nt = seq // r_n
    wblk = W_LRU // W_LRU
    row = lambda s, i: (s, i, 0)
    full2 = lambda s, i: (0, 0)
    full3 = lambda s, i: (0, 0, 0)
    kern = functools.partial(_lru_kernel, sb_n=sb_n, r_n=r_n)
    return pl.pallas_call(
        kern,
        grid=(nseq // sb_n, nt),
        in_specs=[
            pl.BlockSpec((sb_n, r_n, W_LRU), lambda s, i: (s, i, 0)),
            pl.BlockSpec((sb_n, r_n, W_LRU), lambda s, i: (s, i, wblk)),
            pl.BlockSpec((sb_n, SUBLANES, W_LRU), lambda s, i: (s, 0, 0)),
            pl.BlockSpec((sb_n, 1, W_LRU), lambda s, i: (s, 0, 0)),
            pl.BlockSpec((LRU_CONV, W_LRU), full2),
            pl.BlockSpec((1, W_LRU), full2),
            pl.BlockSpec((LRU_BLOCKS, LRU_BS, LRU_BS), full3),
            pl.BlockSpec((1, W_LRU), full2),
            pl.BlockSpec((LRU_BLOCKS, LRU_BS, LRU_BS), full3),
            pl.BlockSpec((1, W_LRU), full2),
            pl.BlockSpec((1, W_LRU), full2),
        ],
        out_specs=[pl.BlockSpec((sb_n, r_n, W_LRU), row),
                   pl.BlockSpec((sb_n, 1, W_LRU), lambda s, i: (s, 0, 0))],
        out_shape=[jax.ShapeDtypeStruct((nseq, seq, W_LRU), BF16),
                   jax.ShapeDtypeStruct((nseq, 1, W_LRU), F32)],
        scratch_shapes=[pltpu.VMEM((SUBLANES, W_LRU), F32), pltpu.VMEM((1, W_LRU), F32)],
        compiler_params=_params(("parallel", "arbitrary")),
        name="lru_mixer",
    )(z3, z3, hist, h0, prm['lru_conv_w'], prm['lru_conv_b'], prm['lru_wr'], prm['lru_br'],
      prm['lru_wi'], prm['lru_bi'], prm['lru_lambda'])


def _hgrn_kernel(zq_ref, zf_ref, zi_ref, zg_ref, s0_ref, lb_ref, nw_ref, yb_ref, sout_ref, st_scr,
                 *, sb_n, r_n, chunk):
    i = pl.program_id(2)
    first = i == 0
    nchunk = r_n // chunk
    tc = lax.broadcasted_iota(jnp.int32, (r_n, 1), 0) % chunk
    lb = lb_ref[...]
    scale = HGRN_KDIM ** -0.5
    for sb in range(sb_n):
        q = zq_ref[sb]
        q = q * _sigmoid(q)
        f = lb + (1.0 - lb) * _sigmoid(zf_ref[sb])
        k = 1.0 - f
        g = jnp.log(f)
        v = zi_ref[sb]
        b = g
        s = 1
        while s < chunk:
            b = jnp.where(tc >= s, b + pltpu.roll(b, s, 0), b)
            s *= 2
        b3 = b.reshape(nchunk, chunk, HGRN_KDIM)
        b_last = jnp.broadcast_to(b3[:, chunk - 1:chunk, :], b3.shape).reshape(r_n, HGRN_KDIM)
        qe = q * jnp.exp(b)
        kd = k * jnp.exp(b_last - b)
        o_in = jnp.zeros_like(q)
        for d in range(chunk):
            if d == 0:
                k_d, b_d, v_d = k, b, v
            else:
                k_d, b_d, v_d = pltpu.roll(k, d, 0), pltpu.roll(b, d, 0), pltpu.roll(v, d, 0)
            valid = tc >= d
            p = jnp.where(valid, q * k_d * jnp.exp(jnp.where(valid, b - b_d, 0.0)), 0.0)
            o_in = o_in + jnp.sum(p, axis=-1, keepdims=True) * v_d
        st = jnp.where(first, s0_ref[sb, 0].T, st_scr[...])
        outs = []
        for c in range(nchunk):
            lo, hi = c * chunk, (c + 1) * chunk
            inter = lax.dot_general(qe[lo:hi].astype(BF16), st.astype(BF16), (((1,), (1,)), ((), ())),
                                    preferred_element_type=F32)
            outs.append((inter + o_in[lo:hi]) * scale)
            upd = lax.dot_general(v[lo:hi].astype(BF16), kd[lo:hi].astype(BF16), (((0,), (0,)), ((), ())),
                                  preferred_element_type=F32)
            st = st * jnp.exp(b[hi - 1:hi]) + upd
        st_scr[...] = st
        sout_ref[sb, 0] = st.T
        o = outs[0] if nchunk == 1 else jnp.concatenate(outs, axis=0)
        o = o * lax.rsqrt(jnp.mean(o * o, axis=-1, keepdims=True) + EPS) * nw_ref[...]
        zg = zg_ref[sb]
        yb_ref[sb] = (o * (zg * _sigmoid(zg))).astype(yb_ref.dtype)


def hgrn_mixer(z3, s0, lb_row, norm_w, *, sb_n, r_n):
    nseq, seq, _ = z3.shape
    assert sb_n == 1 or r_n == seq
    nt = seq // r_n
    chunk = min(HGRN_CHUNK, r_n)
    c0 = 2 * W_LRU // HGRN_KDIM

    def zspec(k):
        return pl.BlockSpec((sb_n, r_n, HGRN_KDIM), lambda s, h, i, k=k: (s, i, c0 + k * HGRN_HEADS + h))

    kern = functools.partial(_hgrn_kernel, sb_n=sb_n, r_n=r_n, chunk=chunk)
    return pl.pallas_call(
        kern,
        grid=(nseq // sb_n, HGRN_HEADS, nt),
        in_specs=[zspec(0), zspec(1), zspec(2), zspec(3),
                  pl.BlockSpec((sb_n, 1, HGRN_KDIM, HGRN_VDIM), lambda s, h, i: (s, h, 0, 0)),
                  pl.BlockSpec((1, HGRN_KDIM), lambda s, h, i: (0, h)),
                  pl.BlockSpec((1, HGRN_VDIM), lambda s, h, i: (0, 0))],
        out_specs=[pl.BlockSpec((sb_n, r_n, HGRN_VDIM), lambda s, h, i: (s, i, h)),
                   pl.BlockSpec((sb_n, 1, HGRN_KDIM, HGRN_VDIM), lambda s, h, i: (s, h, 0, 0))],
        out_shape=[jax.ShapeDtypeStruct((nseq, seq, W_HGRN), BF16),
                   jax.ShapeDtypeStruct((nseq, HGRN_HEADS, HGRN_KDIM, HGRN_VDIM), F32)],
        scratch_shapes=[pltpu.VMEM((HGRN_VDIM, HGRN_KDIM), F32)],
        compiler_params=_params(("parallel", "parallel", "arbitrary")),
        name="hgrn_mixer",
    )(z3, z3, z3, z3, s0, lb_row, norm_w)


def _seg_sum(x, lane_lo):
    s0 = jnp.sum(jnp.where(lane_lo, x, 0.0), axis=-1, keepdims=True)
    s1 = jnp.sum(jnp.where(lane_lo, 0.0, x), axis=-1, keepdims=True)
    return jnp.where(lane_lo, s0, s1)


def _rwkv_kernel(zr_ref, zwd_ref, zk_ref, zv_ref, zad_ref, zgd_ref,
                 shr_ref, shwd_ref, shk_ref, shv_ref, shad_ref, shgd_ref,
                 mur_ref, muwd_ref, muk_ref, muv_ref, muad_ref, mugd_ref,
                 w0_ref, w2_ref, a0_ref, a2_ref, g2_ref, kk_ref, ka_ref, rk_ref, lw_ref, lb_ref,
                 s0_ref, yc_ref, sout_ref,
                 car_r, car_wd, car_k, car_v, car_ad, car_gd, s_scr,
                 x1_scr, r_scr, w_scr, b_scr, k_scr, v_scr, y_scr, *, sb_n, r_n, pg_n):
    i = pl.program_id(2)
    first = i == 0
    row0 = lax.broadcasted_iota(jnp.int32, (r_n, 1), 0) == 0
    lane = lax.broadcasted_iota(jnp.int32, (1, LANES), 1)
    lane_lo = lane < RWKV_HDIM

    def mix(z_ref, sh_ref, mu_ref, car, sb):
        x = z_ref[sb]
        prev0 = jnp.where(first, sh_ref[sb], car[sb])
        prev = jnp.where(row0, prev0, pltpu.roll(x, 1, 0))
        car[sb] = x[r_n - 1:r_n]
        return x + (prev - x) * mu_ref[...]

    for sb in range(sb_n):
        wdm = mix(zwd_ref, shwd_ref, muwd_ref, car_wd, sb)
        adm = mix(zad_ref, shad_ref, muad_ref, car_ad, sb)
        gdm = mix(zgd_ref, shgd_ref, mugd_ref, car_gd, sb)
        rm = mix(zr_ref, shr_ref, mur_ref, car_r, sb)
        km = mix(zk_ref, shk_ref, muk_ref, car_k, sb)
        vm = mix(zv_ref, shv_ref, muv_ref, car_v, sb)
        th = jnp.tanh(wdm).astype(BF16)
        sg = _sigmoid(gdm).astype(BF16)
        wl = w0_ref[...] + jnp.dot(th, w2_ref[...], preferred_element_type=F32)
        w_log = -_softplus(-wl) - 0.5
        decay = jnp.exp(-jnp.exp(w_log))
        aa = _sigmoid(a0_ref[...] + jnp.dot(adm.astype(BF16), a2_ref[...], preferred_element_type=F32))
        gate = jnp.dot(sg, g2_ref[...], preferred_element_type=F32)
        for pg in range(pg_n):
            cs = slice(pg * LANES, (pg + 1) * LANES)
            kr = km[:, cs]
            kkr = kr * kk_ref[:, cs]
            nrm = jnp.sqrt(_seg_sum(kkr * kkr, lane_lo))
            kk = kkr / jnp.maximum(nrm, 1e-12)
            a_p = aa[:, cs]
            kmod = kr * (1.0 + (a_p - 1.0) * ka_ref[:, cs])
            x1_scr[sb, :, cs] = -kk
            b_scr[sb, :, cs] = kk * a_p
            k_scr[sb, :, cs] = kmod
            w_scr[sb, :, cs] = decay[:, cs]
        r_scr[sb] = rm
        v_scr[sb] = vm
        y_scr[sb] = gate

    sub = lax.broadcasted_iota(jnp.int32, (RWKV_HDIM, LANES), 0)
    lane2 = lax.broadcasted_iota(jnp.int32, (RWKV_HDIM, LANES), 1)
    lo2 = lane2 < RWKV_HDIM
    eye_lo = jnp.logical_and(lo2, lane2 == sub)
    eye_hi = jnp.logical_and(jnp.logical_not(lo2), lane2 - RWKV_HDIM == sub)
    eye2 = jnp.logical_or(eye_lo, eye_hi)

    for sb in range(sb_n):
        for pg in range(pg_n):
            s_scr[sb, pg] = jnp.where(first, s0_ref[sb, pg], s_scr[sb, pg])

    def col_bcast(x):
        c0 = jnp.sum(jnp.where(lo2, x, 0.0), axis=-1, keepdims=True)
        c1 = jnp.sum(jnp.where(lo2, 0.0, x), axis=-1, keepdims=True)
        return jnp.where(lo2, c0, c1)

    def group(g, carry):
        rows = pl.ds(pl.multiple_of(g * SUBLANES, SUBLANES), SUBLANES)
        for sb in range(sb_n):
            for pg in range(pg_n):
                cs = slice(pg * LANES, (pg + 1) * LANES)
                x1_8, w_8, b_8, k_8, v_8, r_8 = [ref[sb, rows, cs] for ref in
                                                 (x1_scr, w_scr, b_scr, k_scr, v_scr, r_scr)]
                s = s_scr[sb, pg]
                yrows = []
                for j in range(SUBLANES):
                    row = lambda a: a[j:j + 1]
                    sa = col_bcast(s * row(x1_8))
                    vcol = col_bcast(jnp.where(eye2, row(v_8), 0.0))
                    s = s * row(w_8) + sa * row(b_8) + vcol * row(k_8)
                    ycol = col_bcast(s * row(r_8))
                    yrows.append(jnp.sum(jnp.where(eye2, ycol, 0.0), axis=0, keepdims=True))
                s_scr[sb, pg] = s
                x1_scr[sb, rows, cs] = jnp.concatenate(yrows, axis=0)
        return carry

    lax.fori_loop(0, r_n // SUBLANES, group, 0)

    for sb in range(sb_n):
        for pg in range(pg_n):
            sout_ref[sb, pg] = s_scr[sb, pg]
        y = x1_scr[sb]
        gate = y_scr[sb]
        mu = _seg_sum_wide(y, pg_n) * (1.0 / RWKV_HDIM)
        yc = y - mu
        var = _seg_sum_wide(yc * yc, pg_n) * (1.0 / RWKV_HDIM)
        yn = yc * lax.rsqrt(var + RWKV_GN_EPS) * lw_ref[...] + lb_ref[...]
        rm = r_scr[sb]
        bonus = _seg_sum_wide(rm * k_scr[sb] * rk_ref[...], pg_n) * v_scr[sb]
        yc_ref[sb] = ((yn + bonus) * gate).astype(yc_ref.dtype)


def _seg_sum_wide(x, pg_n):
    lane = lax.broadcasted_iota(jnp.int32, (1, LANES), 1)
    lane_lo = lane < RWKV_HDIM
    parts = [_seg_sum(x[:, pg * LANES:(pg + 1) * LANES], lane_lo) for pg in range(pg_n)]
    return parts[0] if pg_n == 1 else jnp.concatenate(parts, axis=1)


def rwkv_mixer(z3, shift, s0, prm, *, sb_n, r_n, pg_n):
    nseq, seq, _ = z3.shape
    nt = seq // r_n
    wp = pg_n * LANES
    npg = RWKV_PAIRS // pg_n

    def zspec(off, width, per_pair):
        base = off // width
        if per_pair:
            return pl.BlockSpec((sb_n, r_n, width), lambda s, p, i: (s, i, base + p))
        return pl.BlockSpec((sb_n, r_n, width), lambda s, p, i: (s, i, base))

    def shspec(off, width, per_pair):
        base = off // width
        if per_pair:
            return pl.BlockSpec((sb_n, 1, width), lambda s, p, i: (s, 0, base + p))
        return pl.BlockSpec((sb_n, 1, width), lambda s, p, i: (s, 0, base))

    def muspec(off, width, per_pair):
        base = off // width
        if per_pair:
            return pl.BlockSpec((1, width), lambda s, p, i: (0, base + p))
        return pl.BlockSpec((1, width), lambda s, p, i: (0, base))

    def pspec(rows):
        return pl.BlockSpec((rows, wp), lambda s, p, i: (0, p))

    pieces = [(Z_R, wp, True), (Z_WD, D_DECAY, False), (Z_K, wp, True), (Z_V, wp, True),
              (Z_AD, D_AAA, False), (Z_GD, D_GATE_PAD, False)]
    for off, width, _ in pieces:
        assert off % width == 0, (off, width)
    in_specs = ([zspec(*p) for p in pieces] + [shspec(*p) for p in pieces] + [muspec(*p) for p in pieces]
                + [pspec(1), pspec(D_DECAY), pspec(1), pspec(D_AAA), pspec(D_GATE_PAD),
                   pspec(1), pspec(1), pspec(1), pspec(1), pspec(1),
                   pl.BlockSpec((sb_n, pg_n, RWKV_HDIM, LANES), lambda s, p, i: (s, p, 0, 0))])
    kern = functools.partial(_rwkv_kernel, sb_n=sb_n, r_n=r_n, pg_n=pg_n)
    tile = lambda: pltpu.VMEM((sb_n, r_n, wp), F32)
    scratch = ([pltpu.VMEM((sb_n, 1, w), F32) for w in (wp, D_DECAY, wp, wp, D_AAA, D_GATE_PAD)]
               + [pltpu.VMEM((sb_n, pg_n, RWKV_HDIM, LANES), F32)]
               + [tile() for _ in range(7)])
    mu = prm['rwkv_mu']
    return pl.pallas_call(
        kern,
        grid=(nseq // sb_n, npg, nt),
        in_specs=in_specs,
        out_specs=[pl.BlockSpec((sb_n, r_n, wp), lambda s, p, i: (s, i, p)),
                   pl.BlockSpec((sb_n, pg_n, RWKV_HDIM, LANES), lambda s, p, i: (s, p, 0, 0))],
        out_shape=[jax.ShapeDtypeStruct((nseq, seq, W_RWKV), BF16),
                   jax.ShapeDtypeStruct((nseq, RWKV_PAIRS, RWKV_HDIM, LANES), F32)],
        scratch_shapes=scratch,
        compiler_params=_params(("parallel", "parallel", "arbitrary")),
        name="rwkv_mixer",
    )(z3, z3, z3, z3, z3, z3, shift, shift, shift, shift, shift, shift, mu, mu, mu, mu, mu, mu,
      prm['rwkv_w0'], prm['rwkv_w2'], prm['rwkv_a0'], prm['rwkv_a2'], prm['rwkv_g2'],
      prm['rwkv_k_k'], prm['rwkv_k_a'], prm['rwkv_r_k'], prm['rwkv_lnx_w'], prm['rwkv_lnx_b'], s0)


FFN_COL_CHUNK = 256


def _ffn_act_kernel(ug_ref, uv_ref, hg_ref, hv_ref, cwg_ref, cwv_ref, cbg_ref, cbv_ref, o_ref,
                    pg_scr, pv_scr, *, sb_n, r_n):
    i = pl.program_id(1)
    first = i == 0
    t8 = lax.broadcasted_iota(jnp.int32, (r_n, 1), 0) % SUBLANES

    def conv(x, hist8, cw, cb):
        hist = _hist_tile(x, hist8)
        y = cb + cw[FFN_CONV - 1:FFN_CONV] * x
        for j in range(1, FFN_CONV):
            y = y + cw[FFN_CONV - 1 - j:FFN_CONV - j] * _shift_rows(x, hist, j, t8)
        return y

    def body(c, carry):
        cs = pl.ds(pl.multiple_of(c * FFN_COL_CHUNK, FFN_COL_CHUNK), FFN_COL_CHUNK)
        for sb in range(sb_n):
            xg = ug_ref[sb, :, cs]
            xv = uv_ref[sb, :, cs]
            hg8 = jnp.where(first, hg_ref[sb, :, cs], pg_scr[:, cs])
            hv8 = jnp.where(first, hv_ref[sb, :, cs], pv_scr[:, cs])
            g = conv(xg, hg8, cwg_ref[:, cs], cbg_ref[:, cs])
            v = conv(xv, hv8, cwv_ref[:, cs], cbv_ref[:, cs])
            pg_scr[:, cs] = xg[r_n - SUBLANES:]
            pv_scr[:, cs] = xv[r_n - SUBLANES:]
            o_ref[sb, :, cs] = (_gelu_tanh(g) * v).astype(o_ref.dtype)
        return carry

    lax.fori_loop(0, FFN_DIM // FFN_COL_CHUNK, body, 0)


def ffn_act(u3, hist, conv_w, conv_b, *, sb_n, r_n):
    nseq, seq, _ = u3.shape
    assert sb_n == 1 or r_n == seq
    nt = seq // r_n
    kern = functools.partial(_ffn_act_kernel, sb_n=sb_n, r_n=r_n)
    return pl.pallas_call(
        kern,
        grid=(nseq // sb_n, nt),
        in_specs=[pl.BlockSpec((sb_n, r_n, FFN_DIM), lambda s, i: (s, i, 0)),
                  pl.BlockSpec((sb_n, r_n, FFN_DIM), lambda s, i: (s, i, 1)),
                  pl.BlockSpec((sb_n, SUBLANES, FFN_DIM), lambda s, i: (s, 0, 0)),
                  pl.BlockSpec((sb_n, SUBLANES, FFN_DIM), lambda s, i: (s, 0, 1)),
                  pl.BlockSpec((FFN_CONV, FFN_DIM), lambda s, i: (0, 0)),
                  pl.BlockSpec((FFN_CONV, FFN_DIM), lambda s, i: (0, 1)),
                  pl.BlockSpec((1, FFN_DIM), lambda s, i: (0, 0)),
                  pl.BlockSpec((1, FFN_DIM), lambda s, i: (0, 1))],
        out_specs=pl.BlockSpec((sb_n, r_n, FFN_DIM), lambda s, i: (s, i, 0)),
        out_shape=jax.ShapeDtypeStruct((nseq, seq, FFN_DIM), BF16),
        scratch_shapes=[pltpu.VMEM((SUBLANES, FFN_DIM), F32), pltpu.VMEM((SUBLANES, FFN_DIM), F32)],
        compiler_params=_params(("parallel", "arbitrary")),
        name="ffn_act",
    )(u3, u3, hist, hist, conv_w, conv_w, conv_b, conv_b)


def _hist_from_state(buf):
    nseq, kb, c = buf.shape
    return jnp.concatenate([jnp.zeros((nseq, SUBLANES - kb, c), buf.dtype), buf], axis=1)


def _tiles(seq):
    if seq > 256:
        return dict(lru=(1, 256), hgrn=(1, 256), ffn=(1, 64))
    return dict(lru=(8, seq), hgrn=(8, seq), ffn=(8, seq))


def _rwkv_tiles(nseq, seq):
    if seq > 256:
        return dict(sb_n=nseq, r_n=128, pg_n=3)
    return dict(sb_n=8, r_n=seq, pg_n=3)


def _layer(x, p_bf, st, prm, wts, lb_row):
    lru_h, lru_conv, hgrn_s, rwkv_s, rwkv_shift, ffn_conv = st
    nseq, seq, d = x.shape
    n = nseq * seq
    x2 = x.reshape(n, d)
    tl = _tiles(seq)

    h = rmsnorm(x2, prm['attn_norm'], BF16)
    z = matmul(h, wts['w_in'], tm=1024, tn=512, name="in_proj")
    z3 = z.reshape(nseq, seq, N_IN_PAD)

    sb_n, r_n = tl['lru']
    ya, new_lru_h = lru_mixer(z3, _hist_from_state(lru_conv), lru_h.reshape(nseq, 1, W_LRU), wts,
                              sb_n=sb_n, r_n=r_n)
    new_lru_conv = z3[:, seq - (LRU_CONV - 1):, :W_LRU]

    sb_n, r_n = tl['hgrn']
    yb, new_hgrn = hgrn_mixer(z3, hgrn_s, lb_row, wts['hgrn_norm'], sb_n=sb_n, r_n=r_n)

    shift = _zc_to_kernel_layout(rwkv_shift).reshape(nseq, 1, N_IN_PAD)
    s0 =rwkv_s.reshape(nseq, RWKV_PAIRS, 2, RWKV_HDIM, RWKV_HDIM).transpose(0, 1, 3, 2, 4)
    s0 = s0.reshape(nseq, RWKV_PAIRS, RWKV_HDIM, LANES)
    yc, new_rwkv = rwkv_mixer(z3, shift, s0, wts, **_rwkv_tiles(nseq, seq))
    new_rwkv = new_rwkv.reshape(nseq, RWKV_PAIRS, RWKV_HDIM, 2, RWKV_HDIM).transpose(0, 1, 3, 2, 4)
    new_rwkv = new_rwkv.reshape(nseq, RWKV_HEADS, RWKV_HDIM, RWKV_HDIM)
    new_shift = _zc_from_kernel_layout(z3[:, seq - 1, :])

    x2 = outproj(ya.reshape(n, W_LRU), yb.reshape(n, W_HGRN), yc.reshape(n, W_RWKV),
                 wts['w_out_a'], wts['w_out_b'], wts['w_out_c'], x2)

    h2 = rmsnorm(x2, prm['ffn_norm'], BF16)
    u = matmul(h2, wts['ffn_up'], tm=1024, tn=512, name="ffn_up")
    u3 = u.reshape(nseq, seq, 2 * FFN_DIM)
    sb_n, r_n = tl['ffn']
    act = ffn_act(u3, _hist_from_state(ffn_conv), wts['ffn_conv_w'], wts['ffn_conv_b'], sb_n=sb_n, r_n=r_n)
    new_ffn_conv = u3[:, seq - (FFN_CONV - 1):, :]
    x2 = matmul(act.reshape(n, FFN_DIM), wts['ffn_down'], x2, tm=512, tn=256, name="ffn_down")

    h3 = rmsnorm(x2, prm['ple_norm'], BF16)
    x2 = ple(h3, wts['ple_gate'], p_bf.reshape(n, PLE_DIM), wts['ple_proj'], x2)

    new = (new_lru_h.reshape(nseq, W_LRU), new_lru_conv, new_hgrn, new_rwkv, new_shift, new_ffn_conv)
    return x2.reshape(nseq, seq, d), new


def _layer_weights(prm, i):
    w_in_bf = prm['w_in'][i].astype(BF16)
    w_in = lax.dynamic_update_slice_in_dim(_zc_to_kernel_layout(w_in_bf[:, ZC_OFF:]), w_in_bf[:, :ZC_OFF], 0, axis=1)
    w_out = prm['w_out'][i].astype(BF16)
    row = lambda a: a[i].reshape(1, -1)
    mu = _zc_to_kernel_layout(prm['rwkv_mu'][i]).reshape(1, N_IN_PAD)
    return dict(
        w_in=w_in,
        w_out_a=w_out[:W_LRU], w_out_b=w_out[W_LRU:W_LRU + W_HGRN], w_out_c=w_out[W_LRU + W_HGRN:],
        ffn_up=prm['ffn_up'][i].astype(BF16), ffn_down=prm['ffn_down'][i].astype(BF16),
        ple_gate=prm['ple_gate'][i].astype(BF16), ple_proj=prm['ple_proj'][i].astype(BF16),
        lru_conv_w=prm['lru_conv_w'][i], lru_conv_b=row(prm['lru_conv_b']),
        lru_wr=prm['lru_wr'][i].astype(BF16), lru_br=row(prm['lru_br']),
        lru_wi=prm['lru_wi'][i].astype(BF16), lru_bi=row(prm['lru_bi']),
        lru_lambda=row(prm['lru_lambda']),
        hgrn_norm=row(prm['hgrn_norm']),
        rwkv_mu=mu, rwkv_w0=row(prm['rwkv_w0']), rwkv_w2=prm['rwkv_w2'][i].astype(BF16),
        rwkv_a0=row(prm['rwkv_a0']), rwkv_a2=prm['rwkv_a2'][i].astype(BF16),
        rwkv_g2=jnp.pad(prm['rwkv_g2'][i].astype(BF16), ((0, D_GATE_PAD - D_GATE), (0, 0))),
        rwkv_k_k=row(prm['rwkv_k_k']), rwkv_k_a=row(prm['rwkv_k_a']), rwkv_r_k=row(prm['rwkv_r_k']),
        rwkv_lnx_w=row(prm['rwkv_lnx_w']), rwkv_lnx_b=row(prm['rwkv_lnx_b']),
        ffn_conv_w=prm['ffn_conv_w'][i], ffn_conv_b=row(prm['ffn_conv_b']),
    )


def _trunk(x, p, states, prm, layer_wts, lbs, final_norm):
    nseq, seq, d = x.shape
    p_bf = p.astype(BF16)
    outs = [[] for _ in states]
    for i in range(DEPTH):
        norms = {k: prm[k][i] for k in ('attn_norm', 'ffn_norm', 'ple_norm')}
        x, ns = _layer(x, p_bf[i], tuple(s[i] for s in states), norms, layer_wts[i], lbs[i:i + 1])
        for lst, nst in zip(outs, ns):
            lst.append(nst)
    y = rmsnorm(x.reshape(nseq * seq, d), final_norm, F32).reshape(nseq, seq, d)
    return y, tuple(jnp.stack(lst) for lst in outs)


def kernel(x_prompt, x_sample, state_lru_h, state_lru_conv, state_hgrn, state_rwkv, state_rwkv_shift,
           state_ffn_conv, p_prompt, p_sample, attn_norm, w_in, lru_conv_w, lru_conv_b, lru_wr, lru_br,
           lru_wi, lru_bi, lru_lambda, hgrn_lb, hgrn_norm, rwkv_mu, rwkv_w0, rwkv_w2, rwkv_a0, rwkv_a2,
           rwkv_g2, rwkv_k_k, rwkv_k_a, rwkv_r_k, rwkv_lnx_w, rwkv_lnx_b, w_out, ffn_norm, ffn_up,
           ffn_conv_w, ffn_conv_b, ffn_down, ple_norm, ple_gate, ple_proj, final_norm):
    prm = dict(attn_norm=attn_norm, w_in=w_in, lru_conv_w=lru_conv_w, lru_conv_b=lru_conv_b, lru_wr=lru_wr,
               lru_br=lru_br, lru_wi=lru_wi, lru_bi=lru_bi, lru_lambda=lru_lambda, hgrn_norm=hgrn_norm,
               rwkv_mu=rwkv_mu, rwkv_w0=rwkv_w0, rwkv_w2=rwkv_w2, rwkv_a0=rwkv_a0, rwkv_a2=rwkv_a2,
               rwkv_g2=rwkv_g2, rwkv_k_k=rwkv_k_k, rwkv_k_a=rwkv_k_a, rwkv_r_k=rwkv_r_k,
               rwkv_lnx_w=rwkv_lnx_w, rwkv_lnx_b=rwkv_lnx_b, w_out=w_out, ffn_norm=ffn_norm, ffn_up=ffn_up,
               ffn_conv_w=ffn_conv_w, ffn_conv_b=ffn_conv_b, ffn_down=ffn_down, ple_norm=ple_norm,
               ple_gate=ple_gate, ple_proj=ple_proj)
    lbs = hgrn_lower_bounds(hgrn_lb)
    layer_wts = [_layer_weights(prm, i) for i in range(DEPTH)]

    sample_states = (state_lru_h, state_lru_conv, state_hgrn, state_rwkv, state_rwkv_shift, state_ffn_conv)
    bp = x_prompt.shape[0]
    prompt_states = tuple(jnp.zeros((DEPTH, bp) + s.shape[2:], x_prompt.dtype) for s in sample_states)

    y_prompt, st_p = _trunk(x_prompt, p_prompt, prompt_states, prm, layer_wts, lbs, final_norm)
    y_sample, st_s = _trunk(x_sample, p_sample, sample_states, prm, layer_wts, lbs, final_norm)
    return (y_prompt, y_sample) + st_p + st_s
```

```python
import functools

import jax
import jax.numpy as jnp
import numpy as np
from jax import lax
from jax.experimental import pallas as pl
from jax.experimental.pallas import tpu as pltpu

F32 = jnp.float32
BF16 = jnp.bfloat16

D_MODEL = 4096
DEPTH = 4
W_LRU = 1024
LRU_BLOCKS = 8
LRU_BS = 128
LRU_CONV = 4
LRU_C = 8.0
W_HGRN = 1536
HGRN_KDIM = 128
HGRN_HEADS = 12
HGRN_VDIM = 128
HGRN_CHUNK = 16
W_RWKV = 1536
RWKV_HDIM = 64
RWKV_HEADS = 24
RWKV_PAIRS = RWKV_HEADS // 2
D_DECAY = 128
D_AAA = 128
D_GATE = 480
D_GATE_PAD = 512
RWKV_GN_EPS = 64e-5
RWKV_CHUNK = 64
N_RWKV_COLS = 3 * W_RWKV + D_DECAY + D_AAA + D_GATE
N_IN = 2 * W_LRU + 4 * W_HGRN + N_RWKV_COLS
FFN_DIM = 11008
FFN_CONV = 3
PLE_DIM = 256
EPS = 1e-6

LANES = 128
SUBLANES = 8
VMEM_LIMIT_BYTES = 56 * 1024 * 1024

ZC_OFF = 2 * W_LRU + 4 * W_HGRN
Z_WD = ZC_OFF
Z_AD = Z_WD + D_DECAY
Z_R = Z_AD + D_AAA
Z_K = Z_R + W_RWKV
Z_V = Z_K + W_RWKV
Z_GD = 13312
N_IN_PAD = Z_GD + D_GATE_PAD
ZC_REF_ORDER = ((Z_R, W_RWKV), (Z_WD, D_DECAY), (Z_K, W_RWKV), (Z_V, W_RWKV), (Z_AD, D_AAA), (Z_GD, D_GATE))


def _zc_to_kernel_layout(a, head=None):
    zeros = lambda w: jnp.zeros(a.shape[:-1] + (w,), a.dtype)
    placed, pos = {}, 0
    for off, width in ZC_REF_ORDER:
        placed[off] = a[..., pos:pos + width]
        pos += width
    parts, col = [zeros(ZC_OFF) if head is None else head], ZC_OFF
    for off in sorted(placed):
        if off > col:
            parts.append(zeros(off - col))
        parts.append(placed[off])
        col = off + placed[off].shape[-1]
    parts.append(zeros(N_IN_PAD - col))
    return jnp.concatenate(parts, axis=-1)


def _zc_from_kernel_layout(z):
    return jnp.concatenate([z[..., off:off + width] for off, width in ZC_REF_ORDER], axis=-1)


def _params(sem):
    return pltpu.CompilerParams(dimension_semantics=sem, vmem_limit_bytes=VMEM_LIMIT_BYTES)


def _sigmoid(x):
    return 1.0 / (1.0 + jnp.exp(-x))


def _gelu_tanh(x):
    return 0.5 * x * (1.0 + jnp.tanh(0.7978845608028654 * (x + 0.044715 * x * x * x)))


def _softplus(x):
    return jnp.maximum(x, 0.0) + jnp.log1p(jnp.exp(-jnp.abs(x)))


def _rmsnorm_kernel(x_ref, w_ref, o_ref):
    x = x_ref[...]
    ms = jnp.mean(x * x, axis=-1, keepdims=True)
    o_ref[...] = (x * lax.rsqrt(ms + EPS) * w_ref[...]).astype(o_ref.dtype)


def rmsnorm(x, w, out_dtype, tm=256):
    n, d = x.shape
    tm = min(tm, n)
    return pl.pallas_call(
        _rmsnorm_kernel,
        grid=(n // tm,),
        in_specs=[pl.BlockSpec((tm, d), lambda i: (i, 0)), pl.BlockSpec((1, d), lambda i: (0, 0))],
        out_specs=pl.BlockSpec((tm, d), lambda i: (i, 0)),
        out_shape=jax.ShapeDtypeStruct((n, d), out_dtype),
        compiler_params=_params(("parallel",)),
        name="rmsnorm",
    )(x, w.reshape(1, d))


def _mm_kernel(a_ref, w_ref, o_ref):
    o_ref[...] = jnp.dot(a_ref[...], w_ref[...], preferred_element_type=F32)


def _mm_res_kernel(a_ref, w_ref, r_ref, o_ref):
    o_ref[...] = r_ref[...] + jnp.dot(a_ref[...], w_ref[...], preferred_element_type=F32)


def matmul(a, w, res=None, *, tm, tn, name):
    m, k = a.shape
    n = w.shape[1]
    tm = min(tm, m)
    in_specs = [pl.BlockSpec((tm, k), lambda i, j: (i, 0)), pl.BlockSpec((k, tn), lambda i, j: (0, j))]
    args = [a, w]
    kern = _mm_kernel
    if res is not None:
        in_specs.append(pl.BlockSpec((tm, tn), lambda i, j: (i, j)))
        args.append(res)
        kern = _mm_res_kernel
    return pl.pallas_call(
        kern,
        grid=(m // tm, n // tn),
        in_specs=in_specs,
        out_specs=pl.BlockSpec((tm, tn), lambda i, j: (i, j)),
        out_shape=jax.ShapeDtypeStruct((m, n), F32),
        compiler_params=_params(("parallel", "parallel")),
        name=name,
    )(*args)


def _outproj_kernel(a1_ref, a2_ref, a3_ref, w1_ref, w2_ref, w3_ref, r_ref, o_ref):
    acc = jnp.dot(a1_ref[...], w1_ref[...], preferred_element_type=F32)
    acc += jnp.dot(a2_ref[...], w2_ref[...], preferred_element_type=F32)
    acc += jnp.dot(a3_ref[...], w3_ref[...], preferred_element_type=F32)
    o_ref[...] = r_ref[...] + acc


def outproj(ya, yb, yc, w1, w2, w3, res, *, tm=1024, tn=512):
    m = ya.shape[0]
    n = w1.shape[1]
    tm = min(tm, m)
    a_specs = [pl.BlockSpec((tm, a.shape[1]), lambda i, j: (i, 0)) for a in (ya, yb, yc)]
    w_specs = [pl.BlockSpec((w.shape[0], tn), lambda i, j: (0, j)) for w in (w1, w2, w3)]
    return pl.pallas_call(
        _outproj_kernel,
        grid=(m // tm, n // tn),
        in_specs=a_specs + w_specs + [pl.BlockSpec((tm, tn), lambda i, j: (i, j))],
        out_specs=pl.BlockSpec((tm, tn), lambda i, j: (i, j)),
        out_shape=jax.ShapeDtypeStruct((m, n), F32),
        compiler_params=_params(("parallel", "parallel")),
        name="outproj",
    )(ya, yb, yc, w1, w2, w3, res)


def _ple_kernel(h_ref, wg_ref, p_ref, wp_ref, x_ref, o_ref):
    g = _sigmoid(jnp.dot(h_ref[...], wg_ref[...], preferred_element_type=F32))
    pp = jnp.dot(p_ref[...], wp_ref[...], preferred_element_type=F32)
    o_ref[...] = x_ref[...] + pp * g


def ple(h, wg, p, wp, x, *, tm=1024, tn=512):
    m, d = x.shape
    tm = min(tm, m)
    return pl.pallas_call(
        _ple_kernel,
        grid=(m // tm, d // tn),
        in_specs=[pl.BlockSpec((tm, h.shape[1]), lambda i, j: (i, 0)),
                  pl.BlockSpec((wg.shape[0], tn), lambda i, j: (0, j)),
                  pl.BlockSpec((tm, p.shape[1]), lambda i, j: (i, 0)),
                  pl.BlockSpec((wp.shape[0], tn), lambda i, j: (0, j)),
                  pl.BlockSpec((tm, tn), lambda i, j: (i, j))],
        out_specs=pl.BlockSpec((tm, tn), lambda i, j: (i, j)),
        out_shape=jax.ShapeDtypeStruct((m, d), F32),
        compiler_params=_params(("parallel", "parallel")),
        name="ple",
    )(h, wg, p, wp, x)


def _lbs_kernel(lb_ref, o_ref):
    x = lb_ref[...]
    mx = jnp.max(x, axis=0, keepdims=True)
    e = jnp.exp(x - mx)
    sm = e / jnp.sum(e, axis=0, keepdims=True)
    acc = jnp.zeros_like(sm[0:1])
    for l in range(1, DEPTH):
        acc = acc + sm[l:l + 1]
        o_ref[l:l + 1, :] = acc
    o_ref[0:1, :] = jnp.zeros_like(acc)


def hgrn_lower_bounds(hgrn_lb):
    return pl.pallas_call(
        _lbs_kernel,
        out_shape=jax.ShapeDtypeStruct(hgrn_lb.shape, F32),
        name="hgrn_lbs",
    )(hgrn_lb)


def _shift_rows(x, hist, j, t8):
    r = x.shape[0]
    return jnp.where(t8 >= j, pltpu.roll(x, j, 0), pltpu.roll(hist, r - SUBLANES + j, 0))


def _hist_tile(x, hist8):
    r = x.shape[0]
    if r == SUBLANES:
        return hist8
    return jnp.concatenate([hist8, x[:r - SUBLANES]], axis=0)


def _lru_kernel(za_ref, zg_ref, hist_ref, h0_ref, cw_ref, cb_ref, wr_ref, br_ref, wi_ref, bi_ref,
                lam_ref, ya_ref, hout_ref, prev8_scr, hcar_scr, *, sb_n, r_n):
    i = pl.program_id(1)
    first = i == 0
    t8 = lax.broadcasted_iota(jnp.int32, (r_n, 1), 0) % SUBLANES
    lam = lam_ref[...]
    log_sig_lam = -_softplus(-lam)
    cw = cw_ref[...]
    for sb in range(sb_n):
        x = za_ref[sb]
        hist8 = jnp.where(first, hist_ref[sb], prev8_scr[...])
        hist = _hist_tile(x, hist8)
        xa = cb_ref[...] + cw[LRU_CONV - 1:LRU_CONV] * x
        for j in range(1, LRU_CONV):
            xa = xa + cw[LRU_CONV - 1 - j:LRU_CONV - j] * _shift_rows(x, hist, j, t8)
        prev8_scr[...] = x[r_n - SUBLANES:]
        r_lin, i_lin = [], []
        for n in range(LRU_BLOCKS):
            xb = xa[:, n * LRU_BS:(n + 1) * LRU_BS].astype(BF16)
            r_lin.append(jnp.dot(xb, wr_ref[n], preferred_element_type=F32))
            i_lin.append(jnp.dot(xb, wi_ref[n], preferred_element_type=F32))
        rg = _sigmoid(jnp.concatenate(r_lin, axis=1) + br_ref[...])
        ig = _sigmoid(jnp.concatenate(i_lin, axis=1) + bi_ref[...])
        log_a = LRU_C * rg * log_sig_lam
        a = jnp.exp(log_a)
        th = jnp.tanh(log_a)
        u = jnp.sqrt(-2.0 * th / (1.0 - th)) * ig * xa
        for s in (1, 2, 4):
            m = t8 >= s
            a_s = pltpu.roll(a, s, 0)
            u_s = pltpu.roll(u, s, 0)
            u = jnp.where(m, a * u_s + u, u)
            a = jnp.where(m, a * a_s, a)
        hprev = jnp.where(first, h0_ref[sb], hcar_scr[...])
        hs = []
        for g in range(r_n // SUBLANES):
            h = a[g * SUBLANES:(g + 1) * SUBLANES] * hprev + u[g * SUBLANES:(g + 1) * SUBLANES]
            hs.append(h)
            hprev = h[SUBLANES - 1:SUBLANES]
        hall = hs[0] if len(hs) == 1 else jnp.concatenate(hs, axis=0)
        hcar_scr[...] = hprev
        hout_ref[sb] = hprev
        ya_ref[sb] = (hall * _gelu_tanh(zg_ref[sb])).astype(ya_ref.dtype)


def lru_mixer(z3, hist, h0, prm, *, sb_n, r_n):
    nseq, seq, _ = z3.shape
    assert sb_n == 1 or r_n == seq
    nt = seq // r_n
    wblk = W_LRU // W_LRU
    row = lambda s, i: (s, i, 0)
    full2 = lambda s, i: (0, 0)
    full3 = lambda s, i: (0, 0, 0)
    kern = functools.partial(_lru_kernel, sb_n=sb_n, r_n=r_n)
    return pl.pallas_call(
        kern,
        grid=(nseq // sb_n, nt),
        in_specs=[
            pl.BlockSpec((sb_n, r_n, W_LRU), lambda s, i: (s, i, 0)),
            pl.BlockSpec((sb_n, r_n, W_LRU), lambda s, i: (s, i, wblk)),
            pl.BlockSpec((sb_n, SUBLANES, W_LRU), lambda s, i: (s, 0, 0)),
            pl.BlockSpec((sb_n, 1, W_LRU), lambda s, i: (s, 0, 0)),
            pl.BlockSpec((LRU_CONV, W_LRU), full2),
            pl.BlockSpec((1, W_LRU), full2),
            pl.BlockSpec((LRU_BLOCKS, LRU_BS, LRU_BS), full3),
            pl.BlockSpec((1, W_LRU), full2),
            pl.BlockSpec((LRU_BLOCKS, LRU_BS, LRU_BS), full3),
            pl.BlockSpec((1, W_LRU), full2),
            pl.BlockSpec((1, W_LRU), full2),
        ],
        out_specs=[pl.BlockSpec((sb_n, r_n, W_LRU), row),
                   pl.BlockSpec((sb_n, 1, W_LRU), lambda s, i: (s, 0, 0))],
        out_shape=[jax.ShapeDtypeStruct((nseq, seq, W_LRU), BF16),
                   jax.ShapeDtypeStruct((nseq, 1, W_LRU), F32)],
        scratch_shapes=[pltpu.VMEM((SUBLANES, W_LRU), F32), pltpu.VMEM((1, W_LRU), F32)],
        compiler_params=_params(("parallel", "arbitrary")),
        name="lru_mixer",
    )(z3, z3, hist, h0, prm['lru_conv_w'], prm['lru_conv_b'], prm['lru_wr'], prm['lru_br'],
      prm['lru_wi'], prm['lru_bi'], prm['lru_lambda'])


def _hgrn_kernel(zq_ref, zf_ref, zi_ref, zg_ref, s0_ref, lb_ref, nw_ref, yb_ref, sout_ref, st_scr,
                 *, sb_n, r_n, chunk):
    i = pl.program_id(2)
    first = i == 0
    nchunk = r_n // chunk
    tc = lax.broadcasted_iota(jnp.int32, (r_n, 1), 0) % chunk
    lb = lb_ref[...]
    scale = HGRN_KDIM ** -0.5
    for sb in range(sb_n):
        q = zq_ref[sb]
        q = q * _sigmoid(q)
        f = lb + (1.0 - lb) * _sigmoid(zf_ref[sb])
        k = 1.0 - f
        g = jnp.log(f)
        v = zi_ref[sb]
        b = g
        s = 1
        while s < chunk:
            b = jnp.where(tc >= s, b + pltpu.roll(b, s, 0), b)
            s *= 2
        b3 = b.reshape(nchunk, chunk, HGRN_KDIM)
        b_last = jnp.broadcast_to(b3[:, chunk - 1:chunk, :], b3.shape).reshape(r_n, HGRN_KDIM)
        qe = q * jnp.exp(b)
        kd = k * jnp.exp(b_last - b)
        o_in = jnp.zeros_like(q)
        for d in range(chunk):
            if d == 0:
                k_d, b_d, v_d = k, b, v
            else:
                k_d, b_d, v_d = pltpu.roll(k, d, 0), pltpu.roll(b, d, 0), pltpu.roll(v, d, 0)
            valid = tc >= d
            p = jnp.where(valid, q * k_d * jnp.exp(jnp.where(valid, b - b_d, 0.0)), 0.0)
            o_in = o_in + jnp.sum(p, axis=-1, keepdims=True) * v_d
        st = jnp.where(first, s0_ref[sb, 0].T, st_scr[...])
        outs = []
        for c in range(nchunk):
            lo, hi = c * chunk, (c + 1) * chunk
            inter = lax.dot_general(qe[lo:hi].astype(BF16), st.astype(BF16), (((1,), (1,)), ((), ())),
                                    preferred_element_type=F32)
            outs.append((inter + o_in[lo:hi]) * scale)
            upd = lax.dot_general(v[lo:hi].astype(BF16), kd[lo:hi].astype(BF16), (((0,), (0,)), ((), ())),
                                  preferred_element_type=F32)
            st = st * jnp.exp(b[hi - 1:hi]) + upd
        st_scr[...] = st
        sout_ref[sb, 0] = st.T
        o = outs[0] if nchunk == 1 else jnp.concatenate(outs, axis=0)
        o = o * lax.rsqrt(jnp.mean(o * o, axis=-1, keepdims=True) + EPS) * nw_ref[...]
        zg = zg_ref[sb]
        yb_ref[sb] = (o * (zg * _sigmoid(zg))).astype(yb_ref.dtype)


def hgrn_mixer(z3, s0, lb_row, norm_w, *, sb_n, r_n):
    nseq, seq, _ = z3.shape
    assert sb_n == 1 or r_n == seq
    nt = seq // r_n
    chunk = min(HGRN_CHUNK, r_n)
    c0 = 2 * W_LRU // HGRN_KDIM

    def zspec(k):
        return pl.BlockSpec((sb_n, r_n, HGRN_KDIM), lambda s, h, i, k=k: (s, i, c0 + k * HGRN_HEADS + h))

    kern = functools.partial(_hgrn_kernel, sb_n=sb_n, r_n=r_n, chunk=chunk)
    return pl.pallas_call(
        kern,
        grid=(nseq // sb_n, HGRN_HEADS, nt),
        in_specs=[zspec(0), zspec(1), zspec(2), zspec(3),
                  pl.BlockSpec((sb_n, 1, HGRN_KDIM, HGRN_VDIM), lambda s, h, i: (s, h, 0, 0)),
                  pl.BlockSpec((1, HGRN_KDIM), lambda s, h, i: (0, h)),
                  pl.BlockSpec((1, HGRN_VDIM), lambda s, h, i: (0, 0))],
        out_specs=[pl.BlockSpec((sb_n, r_n, HGRN_VDIM), lambda s, h, i: (s, i, h)),
                   pl.BlockSpec((sb_n, 1, HGRN_KDIM, HGRN_VDIM), lambda s, h, i: (s, h, 0, 0))],
        out_shape=[jax.ShapeDtypeStruct((nseq, seq, W_HGRN), BF16),
                   jax.ShapeDtypeStruct((nseq, HGRN_HEADS, HGRN_KDIM, HGRN_VDIM), F32)],
        scratch_shapes=[pltpu.VMEM((HGRN_VDIM, HGRN_KDIM), F32)],
        compiler_params=_params(("parallel", "parallel", "arbitrary")),
        name="hgrn_mixer",
    )(z3, z3, z3, z3, s0, lb_row, norm_w)


def _seg_sum(x, lane_lo):
    s0 = jnp.sum(jnp.where(lane_lo, x, 0.0), axis=-1, keepdims=True)
    s1 = jnp.sum(jnp.where(lane_lo, 0.0, x), axis=-1, keepdims=True)
    return jnp.where(lane_lo, s0, s1)


def _rwkv_kernel(zr_ref, zwd_ref, zk_ref, zv_ref, zad_ref, zgd_ref,
                 shr_ref, shwd_ref, shk_ref, shv_ref, shad_ref, shgd_ref,
                 mur_ref, muwd_ref, muk_ref, muv_ref, muad_ref, mugd_ref,
                 w0_ref, w2_ref, a0_ref, a2_ref, g2_ref, kk_ref, ka_ref, rk_ref, lw_ref, lb_ref,
                 s0_ref, yc_ref, sout_ref,
                 car_r, car_wd, car_k, car_v, car_ad, car_gd, s_scr, *, sb_n, r_n, pg_n, chunk):
    i = pl.program_id(2)
    first = i == 0
    nchunk = r_n // chunk
    row0 = lax.broadcasted_iota(jnp.int32, (r_n, 1), 0) == 0
    tc = lax.broadcasted_iota(jnp.int32, (r_n, 1), 0) % chunk
    lane = lax.broadcasted_iota(jnp.int32, (1, LANES), 1)
    lane_lo = lane < RWKV_HDIM

    def mix(z_ref, sh_ref, mu_ref, car, sb):
        x = z_ref[sb]
        prev0 = jnp.where(first, sh_ref[sb], car[sb])
        prev = jnp.where(row0, prev0, pltpu.roll(x, 1, 0))
        car[sb] = x[r_n - 1:r_n]
        return x + (prev - x) * mu_ref[...]

    wp = pg_n * LANES
    seqs = []
    for sb in range(sb_n):
        wdm = mix(zwd_ref, shwd_ref, muwd_ref, car_wd, sb)
        adm = mix(zad_ref, shad_ref, muad_ref, car_ad, sb)
        gdm = mix(zgd_ref, shgd_ref, mugd_ref, car_gd, sb)
        rm = mix(zr_ref, shr_ref, mur_ref, car_r, sb)
        km = mix(zk_ref, shk_ref, muk_ref, car_k, sb)
        vm = mix(zv_ref, shv_ref, muv_ref, car_v, sb)
        th = jnp.tanh(wdm).astype(BF16)
        sg = _sigmoid(gdm).astype(BF16)
        wl = w0_ref[...] + jnp.dot(th, w2_ref[...], preferred_element_type=F32)
        w_log = -_softplus(-wl) - 0.5
        lw = -jnp.exp(w_log)
        aa = _sigmoid(a0_ref[...] + jnp.dot(adm.astype(BF16), a2_ref[...], preferred_element_type=F32))
        gate = jnp.dot(sg, g2_ref[...], preferred_element_type=F32)
        cw = lw
        s = 1
        while s < chunk:
            cw = jnp.where(tc >= s, cw + pltpu.roll(cw, s, 0), cw)
            s *= 2
        cw3 = cw.reshape(nchunk, chunk, wp)
        cw_last = jnp.broadcast_to(cw3[:, chunk - 1:chunk, :], cw3.shape).reshape(r_n, wp)
        to_end = jnp.exp(cw_last - cw)
        w_fwd = jnp.exp(cw)
        w_inv = jnp.exp(-cw)
        w_prev = jnp.exp(cw - lw)
        kmods, ops = [], []
        for pg in range(pg_n):
            cs = slice(pg * LANES, (pg + 1) * LANES)
            kr = km[:, cs]
            kkr = kr * kk_ref[:, cs]
            nrm = jnp.sqrt(_seg_sum(kkr * kkr, lane_lo))
            kk = kkr / jnp.maximum(nrm, 1e-12)
            a_p = aa[:, cs]
            kmod = kr * (1.0 + (a_p - 1.0) * ka_ref[:, cs])
            kmods.append(kmod)
            beta = kk * a_p
            ops.append(dict(
                a_t=-kk * w_prev[:, cs],
                b_t=beta * w_inv[:, cs],
                k_t=kmod * w_inv[:, cs],
                r_t=rm[:, cs] * w_fwd[:, cs],
                v=vm[:, cs],
                b_end=beta * to_end[:, cs],
                k_end=kmod * to_end[:, cs],
                w_end=jnp.exp(cw_last[:, cs])))
        kmod_w = kmods[0] if pg_n == 1 else jnp.concatenate(kmods, axis=1)
        seqs.append(dict(ops=ops, rm=rm, vm=vm, gate=gate, kmod=kmod_w))

    states = {}
    for sb in range(sb_n):
        for pg in range(pg_n):
            states[sb, pg] = jnp.where(first, _pair_to_blockdiag(s0_ref[sb, pg], lane_lo), s_scr[sb, pg])

    names = ('a_t', 'b_t', 'k_t', 'r_t', 'v', 'b_end', 'k_end')
    y_out = {}
    if chunk == RWKV_CHUNK:
        for c in range(nchunk):
            rs = slice(c * chunk, (c + 1) * chunk)
            units = []
            for sb in range(sb_n):
                for pg in range(pg_n):
                    op = seqs[sb]['ops'][pg]
                    u = {n: op[n][rs] for n in names}
                    u['w_end'] = [op['w_end'][c * chunk:c * chunk + 1]]
                    u['keys'] = [(sb, pg)]
                    units.append(u)
            ys = _rwkv_units(units, states, lane_lo, chunk)
            for u, y in zip(units, ys):
                y_out.setdefault(u['keys'][0], []).append(y)
    else:
        group = RWKV_CHUNK // chunk
        assert nchunk == 1 and sb_n % group == 0
        units = []
        for sg in range(sb_n // group):
            for pg in range(pg_n):
                members = [seqs[sg * group + g]['ops'][pg] for g in range(group)]
                u = {n: jnp.concatenate([m[n] for m in members], axis=0) for n in names}
                u['w_end'] = [m['w_end'][0:1] for m in members]
                u['keys'] = [(sg * group + g, pg) for g in range(group)]
                units.append(u)
        ys = _rwkv_units(units, states, lane_lo, chunk)
        for u, y in zip(units, ys):
            for g, key in enumerate(u['keys']):
                y_out[key] = [y[g * chunk:(g + 1) * chunk]]

    for sb in range(sb_n):
        for pg in range(pg_n):
            st = states[sb, pg]
            s_scr[sb, pg] = st
            sout_ref[sb, pg] = st[:RWKV_HDIM] + st[RWKV_HDIM:]
        cols = [ys[0] if len(ys) == 1 else jnp.concatenate(ys, axis=0) for ys in (y_out[sb, pg] for pg in range(pg_n))]
        y = cols[0] if pg_n == 1 else jnp.concatenate(cols, axis=1)
        sq = seqs[sb]
        mu = _seg_sum_wide(y, pg_n) * (1.0 / RWKV_HDIM)
        yc = y - mu
        var = _seg_sum_wide(yc * yc, pg_n) * (1.0 / RWKV_HDIM)
        yn = yc * lax.rsqrt(var + RWKV_GN_EPS) * lw_ref[...] + lb_ref[...]
        bonus = _seg_sum_wide(sq['rm'] * sq['kmod'] * rk_ref[...], pg_n) * sq['vm']
        yc_ref[sb] = ((yn + bonus) * sq['gate']).astype(yc_ref.dtype)


def _pair_to_blockdiag(s_pair, lane_lo):
    return jnp.concatenate([jnp.where(lane_lo, s_pair, 0.0), jnp.where(lane_lo, 0.0, s_pair)], axis=0)


def _split_bf16(x):
    hi = x.astype(BF16)
    lo = (x - hi.astype(F32)).astype(BF16)
    return hi, lo


def _dot3(a, b, ca, cb):
    a3 = jnp.concatenate([a[0], a[0], a[1]], axis=ca)
    b3 = jnp.concatenate([b[0], b[1], b[0]], axis=cb)
    return lax.dot_general(a3, b3, (((ca,), (cb,)), ((), ())), preferred_element_type=F32)


def _stack_heads(x, lane_lo):
    return jnp.concatenate([jnp.where(lane_lo, x, 0.0), jnp.where(lane_lo, 0.0, x)], axis=0)


def _rwkv_units(units, states, lane_lo, seq_rows):
    n = RWKV_CHUNK
    rows = lax.broadcasted_iota(jnp.int32, (n, LANES), 0)
    s_idx = lax.broadcasted_iota(jnp.int32, (n, LANES), 1) % RWKV_HDIM
    same_seq = (rows // seq_rows) == (s_idx // seq_rows)
    strict = jnp.logical_and(same_seq, s_idx < rows)
    incl = jnp.logical_and(same_seq, s_idx <= rows)
    r2 = lax.broadcasted_iota(jnp.int32, (LANES, LANES), 0)
    c2 = lax.broadcasted_iota(jnp.int32, (LANES, LANES), 1)
    eye = jnp.where(r2 == c2, 1.0, 0.0)
    same_head = (r2 < RWKV_HDIM) == (c2 < RWKV_HDIM)

    for u in units:
        u['ar'] = _split_bf16(jnp.concatenate([u['a_t'], u['r_t']], axis=0))
        pb = _dot3(u['ar'], _split_bf16(_stack_heads(u['b_t'], lane_lo)), 1, 1)
        pk = _dot3(u['ar'], _split_bf16(_stack_heads(u['k_t'], lane_lo)), 1, 1)
        u['n_b'] = jnp.where(incl, pb[n:], 0.0)
        u['mn_k'] = jnp.concatenate([jnp.where(strict, pk[:n], 0.0), jnp.where(incl, pk[n:], 0.0)], axis=0)
        x = _stack_heads(jnp.where(strict, pb[:n], 0.0), lane_lo)
        u['t_acc'] = eye + x
        u['xs'] = _split_bf16(x)
    span = 2
    while span < seq_rows:
        for u in units:
            u['xs'] = _split_bf16(_dot3(u['xs'], u['xs'], 1, 0))
        for u in units:
            u['t_acc'] = u['t_acc'] + _dot3(_split_bf16(u['t_acc']), u['xs'], 1, 0)
        span *= 2
    for u in units:
        u['t_p'] = _split_bf16(u['t_acc'][:n] + u['t_acc'][n:])
        u['zy'] = _dot3(_split_bf16(u['mn_k']), _split_bf16(_stack_heads(u['v'], lane_lo)), 1, 0)
    for u in units:
        if len(u['keys']) == 1:
            ar0 = _dot3(u['ar'], _split_bf16(states[u['keys'][0]]), 1, 1)
            u['a0'], u['r0'] = ar0[:n], ar0[n:]
        else:
            a0, r0 = [], []
            for g, key in enumerate(u['keys']):
                gs = slice(g * seq_rows, (g + 1) * seq_rows)
                ar_g = _split_bf16(jnp.concatenate([u['a_t'][gs], u['r_t'][gs]], axis=0))
                o = _dot3(ar_g, _split_bf16(states[key]), 1, 1)
                a0.append(o[:seq_rows])
                r0.append(o[seq_rows:])
            u['a0'], u['r0'] = jnp.concatenate(a0, axis=0), jnp.concatenate(r0, axis=0)
    for u in units:
        u['u'] = _dot3(u['t_p'], _split_bf16(_stack_heads(u['a0'] + u['zy'][:n], lane_lo)), 1, 0)
    ys = []
    for u in units:
        ys.append(u['r0'] + u['zy'][n:]
                  + _dot3(_split_bf16(u['n_b']), _split_bf16(_stack_heads(u['u'], lane_lo)), 1, 0))
        for g, key in enumerate(u['keys']):
            gs = slice(g * seq_rows, (g + 1) * seq_rows)
            upd = _dot3(_split_bf16(jnp.concatenate([u['u'][gs], u['v'][gs]], axis=0)),
                        _split_bf16(jnp.concatenate([u['b_end'][gs], u['k_end'][gs]], axis=0)), 0, 0)
            states[key] = states[key] * u['w_end'][g] + jnp.where(same_head, upd, 0.0)
    return ys


def _seg_sum_wide(x, pg_n):
    lane = lax.broadcasted_iota(jnp.int32, (1, LANES), 1)
    lane_lo = lane < RWKV_HDIM
    parts = [_seg_sum(x[:, pg * LANES:(pg + 1) * LANES], lane_lo) for pg in range(pg_n)]
    return parts[0] if pg_n == 1 else jnp.concatenate(parts, axis=1)


def rwkv_mixer(z3, shift, s0, prm, *, sb_n, r_n, pg_n):
    nseq, seq, _ = z3.shape
    nt = seq // r_n
    wp = pg_n * LANES
    npg = RWKV_PAIRS // pg_n

    def zspec(off, width, per_pair):
        base = off // width
        if per_pair:
            return pl.BlockSpec((sb_n, r_n, width), lambda s, p, i: (s, i, base + p))
        return pl.BlockSpec((sb_n, r_n, width), lambda s, p, i: (s, i, base))

    def shspec(off, width, per_pair):
        base = off // width
        if per_pair:
            return pl.BlockSpec((sb_n, 1, width), lambda s, p, i: (s, 0, base + p))
        return pl.BlockSpec((sb_n, 1, width), lambda s, p, i: (s, 0, base))

    def muspec(off, width, per_pair):
        base = off // width
        if per_pair:
            return pl.BlockSpec((1, width), lambda s, p, i: (0, base + p))
        return pl.BlockSpec((1, width), lambda s, p, i: (0, base))

    def pspec(rows):
        return pl.BlockSpec((rows, wp), lambda s, p, i: (0, p))

    pieces = [(Z_R, wp, True), (Z_WD, D_DECAY, False), (Z_K, wp, True), (Z_V, wp, True),
              (Z_AD, D_AAA, False), (Z_GD, D_GATE_PAD, False)]
    for off, width, _ in pieces:
        assert off % width == 0, (off, width)
    in_specs = ([zspec(*p) for p in pieces] + [shspec(*p) for p in pieces] + [muspec(*p) for p in pieces]
                + [pspec(1), pspec(D_DECAY), pspec(1), pspec(D_AAA), pspec(D_GATE_PAD),
                   pspec(1), pspec(1), pspec(1), pspec(1), pspec(1),
                   pl.BlockSpec((sb_n, pg_n, RWKV_HDIM, LANES), lambda s, p, i: (s, p, 0, 0))])
    chunk = min(RWKV_CHUNK, r_n)
    kern = functools.partial(_rwkv_kernel, sb_n=sb_n, r_n=r_n, pg_n=pg_n, chunk=chunk)
    scratch = ([pltpu.VMEM((sb_n, 1, w), F32) for w in (wp, D_DECAY, wp, wp, D_AAA, D_GATE_PAD)]
               + [pltpu.VMEM((sb_n, pg_n, LANES, LANES), F32)])
    mu = prm['rwkv_mu']
    return pl.pallas_call(
        kern,
        grid=(nseq // sb_n, npg, nt),
        in_specs=in_specs,
        out_specs=[pl.BlockSpec((sb_n, r_n, wp), lambda s, p, i: (s, i, p)),
                   pl.BlockSpec((sb_n, pg_n, RWKV_HDIM, LANES), lambda s, p, i: (s, p, 0, 0))],
        out_shape=[jax.ShapeDtypeStruct((nseq, seq, W_RWKV), BF16),
                   jax.ShapeDtypeStruct((nseq, RWKV_PAIRS, RWKV_HDIM, LANES), F32)],
        scratch_shapes=scratch,
        compiler_params=_params(("parallel", "parallel", "arbitrary")),
        name="rwkv_mixer",
    )(z3, z3, z3, z3, z3, z3, shift, shift, shift, shift, shift, shift, mu, mu, mu, mu, mu, mu,
      prm['rwkv_w0'], prm['rwkv_w2'], prm['rwkv_a0'], prm['rwkv_a2'], prm['rwkv_g2'],
      prm['rwkv_k_k'], prm['rwkv_k_a'], prm['rwkv_r_k'], prm['rwkv_lnx_w'], prm['rwkv_lnx_b'], s0)


FFN_COL_CHUNK = 256


def _ffn_act_kernel(ug_ref, uv_ref, hg_ref, hv_ref, cwg_ref, cwv_ref, cbg_ref, cbv_ref, o_ref,
                    pg_scr, pv_scr, *, sb_n, r_n):
    i = pl.program_id(1)
    first = i == 0
    t8 = lax.broadcasted_iota(jnp.int32, (r_n, 1), 0) % SUBLANES

    def conv(x, hist8, cw, cb):
        hist = _hist_tile(x, hist8)
        y = cb + cw[FFN_CONV - 1:FFN_CONV] * x
        for j in range(1, FFN_CONV):
            y = y + cw[FFN_CONV - 1 - j:FFN_CONV - j] * _shift_rows(x, hist, j, t8)
        return y

    def body(c, carry):
        cs = pl.ds(pl.multiple_of(c * FFN_COL_CHUNK, FFN_COL_CHUNK), FFN_COL_CHUNK)
        for sb in range(sb_n):
            xg = ug_ref[sb, :, cs]
            xv = uv_ref[sb, :, cs]
            hg8 = jnp.where(first, hg_ref[sb, :, cs], pg_scr[:, cs])
            hv8 = jnp.where(first, hv_ref[sb, :, cs], pv_scr[:, cs])
            g = conv(xg, hg8, cwg_ref[:, cs], cbg_ref[:, cs])
            v = conv(xv, hv8, cwv_ref[:, cs], cbv_ref[:, cs])
            pg_scr[:, cs] = xg[r_n - SUBLANES:]
            pv_scr[:, cs] = xv[r_n - SUBLANES:]
            o_ref[sb, :, cs] = (_gelu_tanh(g) * v).astype(o_ref.dtype)
        return carry

    lax.fori_loop(0, FFN_DIM // FFN_COL_CHUNK, body, 0)


def ffn_act(u3, hist, conv_w, conv_b, *, sb_n, r_n):
    nseq, seq, _ = u3.shape
    assert sb_n == 1 or r_n == seq
    nt = seq // r_n
    kern = functools.partial(_ffn_act_kernel, sb_n=sb_n, r_n=r_n)
    return pl.pallas_call(
        kern,
        grid=(nseq // sb_n, nt),
        in_specs=[pl.BlockSpec((sb_n, r_n, FFN_DIM), lambda s, i: (s, i, 0)),
                  pl.BlockSpec((sb_n, r_n, FFN_DIM), lambda s, i: (s, i, 1)),
                  pl.BlockSpec((sb_n, SUBLANES, FFN_DIM), lambda s, i: (s, 0, 0)),
                  pl.BlockSpec((sb_n, SUBLANES, FFN_DIM), lambda s, i: (s, 0, 1)),
                  pl.BlockSpec((FFN_CONV, FFN_DIM), lambda s, i: (0, 0)),
                  pl.BlockSpec((FFN_CONV, FFN_DIM), lambda s, i: (0, 1)),
                  pl.BlockSpec((1, FFN_DIM), lambda s, i: (0, 0)),
                  pl.BlockSpec((1, FFN_DIM), lambda s, i: (0, 1))],
        out_specs=pl.BlockSpec((sb_n, r_n, FFN_DIM), lambda s, i: (s, i, 0)),
        out_shape=jax.ShapeDtypeStruct((nseq, seq, FFN_DIM), BF16),
        scratch_shapes=[pltpu.VMEM((SUBLANES, FFN_DIM), F32), pltpu.VMEM((SUBLANES, FFN_DIM), F32)],
        compiler_params=_params(("parallel", "arbitrary")),
        name="ffn_act",
    )(u3, u3, hist, hist, conv_w, conv_w, conv_b, conv_b)


def _hist_from_state(buf):
    nseq, kb, c = buf.shape
    return jnp.concatenate([jnp.zeros((nseq, SUBLANES - kb, c), buf.dtype), buf], axis=1)


def _tiles(seq):
    if seq > 256:
        return dict(lru=(1, 256), hgrn=(1, 256), ffn=(1, 64))
    return dict(lru=(8, seq), hgrn=(8, seq), ffn=(8, seq))


def _rwkv_tiles(nseq, seq):
    if seq > 256:
        return dict(sb_n=nseq, r_n=128, pg_n=3)
    return dict(sb_n=8, r_n=seq, pg_n=3)


def _layer(x, p_bf, st, prm, wts, lb_row):
    lru_h, lru_conv, hgrn_s, rwkv_s, rwkv_shift, ffn_conv = st
    nseq, seq, d = x.shape
    n = nseq * seq
    x2 = x.reshape(n, d)
    tl = _tiles(seq)

    h = rmsnorm(x2, prm['attn_norm'], BF16)
    z = matmul(h, wts['w_in'], tm=1024, tn=512, name="in_proj")
    z3 = z.reshape(nseq, seq, N_IN_PAD)

    sb_n, r_n = tl['lru']
    ya, new_lru_h = lru_mixer(z3, _hist_from_state(lru_conv), lru_h.reshape(nseq, 1, W_LRU), wts,
                              sb_n=sb_n, r_n=r_n)
    new_lru_conv = z3[:, seq - (LRU_CONV - 1):, :W_LRU]

    sb_n, r_n = tl['hgrn']
    yb, new_hgrn = hgrn_mixer(z3, hgrn_s, lb_row, wts['hgrn_norm'], sb_n=sb_n, r_n=r_n)

    shift = _zc_to_kernel_layout(rwkv_shift).reshape(nseq, 1, N_IN_PAD)
    s0 =rwkv_s.reshape(nseq, RWKV_PAIRS, 2, RWKV_HDIM, RWKV_HDIM).transpose(0, 1, 3, 2, 4)
    s0 = s0.reshape(nseq, RWKV_PAIRS, RWKV_HDIM, LANES)
    yc, new_rwkv = rwkv_mixer(z3, shift, s0, wts, **_rwkv_tiles(nseq, seq))
    new_rwkv = new_rwkv.reshape(nseq, RWKV_PAIRS, RWKV_HDIM, 2, RWKV_HDIM).transpose(0, 1, 3, 2, 4)
    new_rwkv = new_rwkv.reshape(nseq, RWKV_HEADS, RWKV_HDIM, RWKV_HDIM)
    new_shift = _zc_from_kernel_layout(z3[:, seq - 1, :])

    x2 = outproj(ya.reshape(n, W_LRU), yb.reshape(n, W_HGRN), yc.reshape(n, W_RWKV),
                 wts['w_out_a'], wts['w_out_b'], wts['w_out_c'], x2)

    h2 = rmsnorm(x2, prm['ffn_norm'], BF16)
    u = matmul(h2, wts['ffn_up'], tm=1024, tn=512, name="ffn_up")
    u3 = u.reshape(nseq, seq, 2 * FFN_DIM)
    sb_n, r_n = tl['ffn']
    act = ffn_act(u3, _hist_from_state(ffn_conv), wts['ffn_conv_w'], wts['ffn_conv_b'], sb_n=sb_n, r_n=r_n)
    new_ffn_conv = u3[:, seq - (FFN_CONV - 1):, :]
    x2 = matmul(act.reshape(n, FFN_DIM), wts['ffn_down'], x2, tm=512, tn=256, name="ffn_down")

    h3 = rmsnorm(x2, prm['ple_norm'], BF16)
    x2 = ple(h3, wts['ple_gate'], p_bf.reshape(n, PLE_DIM), wts['ple_proj'], x2)

    new = (new_lru_h.reshape(nseq, W_LRU), new_lru_conv, new_hgrn, new_rwkv, new_shift, new_ffn_conv)
    return x2.reshape(nseq, seq, d), new


def _layer_weights(prm, i):
    w_in_bf = prm['w_in'][i].astype(BF16)
    w_in = _zc_to_kernel_layout(w_in_bf[:, ZC_OFF:], head=w_in_bf[:, :ZC_OFF])
    w_out = prm['w_out'][i].astype(BF16)
    row = lambda a: a[i].reshape(1, -1)
    mu = _zc_to_kernel_layout(prm['rwkv_mu'][i]).reshape(1, N_IN_PAD)
    return dict(
        w_in=w_in,
        w_out_a=w_out[:W_LRU], w_out_b=w_out[W_LRU:W_LRU + W_HGRN], w_out_c=w_out[W_LRU + W_HGRN:],
        ffn_up=prm['ffn_up'][i].astype(BF16), ffn_down=prm['ffn_down'][i].astype(BF16),
        ple_gate=prm['ple_gate'][i].astype(BF16), ple_proj=prm['ple_proj'][i].astype(BF16),
        lru_conv_w=prm['lru_conv_w'][i], lru_conv_b=row(prm['lru_conv_b']),
        lru_wr=prm['lru_wr'][i].astype(BF16), lru_br=row(prm['lru_br']),
        lru_wi=prm['lru_wi'][i].astype(BF16), lru_bi=row(prm['lru_bi']),
        lru_lambda=row(prm['lru_lambda']),
        hgrn_norm=row(prm['hgrn_norm']),
        rwkv_mu=mu, rwkv_w0=row(prm['rwkv_w0']), rwkv_w2=prm['rwkv_w2'][i].astype(BF16),
        rwkv_a0=row(prm['rwkv_a0']), rwkv_a2=prm['rwkv_a2'][i].astype(BF16),
        rwkv_g2=jnp.pad(prm['rwkv_g2'][i].astype(BF16), ((0, D_GATE_PAD - D_GATE), (0, 0))),
        rwkv_k_k=row(prm['rwkv_k_k']), rwkv_k_a=row(prm['rwkv_k_a']), rwkv_r_k=row(prm['rwkv_r_k']),
        rwkv_lnx_w=row(prm['rwkv_lnx_w']), rwkv_lnx_b=row(prm['rwkv_lnx_b']),
        ffn_conv_w=prm['ffn_conv_w'][i], ffn_conv_b=row(prm['ffn_conv_b']),
    )


def _trunk(x, p, states, prm, layer_wts, lbs, final_norm):
    nseq, seq, d = x.shape
    p_bf = p.astype(BF16)
    outs = [[] for _ in states]
    for i in range(DEPTH):
        norms = {k: prm[k][i] for k in ('attn_norm', 'ffn_norm', 'ple_norm')}
        x, ns = _layer(x, p_bf[i], tuple(s[i] for s in states), norms, layer_wts[i], lbs[i:i + 1])
        for lst, nst in zip(outs, ns):
            lst.append(nst)
    y = rmsnorm(x.reshape(nseq * seq, d), final_norm, F32).reshape(nseq, seq, d)
    return y, tuple(jnp.stack(lst) for lst in outs)


def kernel(x_prompt, x_sample, state_lru_h, state_lru_conv, state_hgrn, state_rwkv, state_rwkv_shift,
           state_ffn_conv, p_prompt, p_sample, attn_norm, w_in, lru_conv_w, lru_conv_b, lru_wr, lru_br,
           lru_wi, lru_bi, lru_lambda, hgrn_lb, hgrn_norm, rwkv_mu, rwkv_w0, rwkv_w2, rwkv_a0, rwkv_a2,
           rwkv_g2, rwkv_k_k, rwkv_k_a, rwkv_r_k, rwkv_lnx_w, rwkv_lnx_b, w_out, ffn_norm, ffn_up,
           ffn_conv_w, ffn_conv_b, ffn_down, ple_norm, ple_gate, ple_proj, final_norm):
    prm = dict(attn_norm=attn_norm, w_in=w_in, lru_conv_w=lru_conv_w, lru_conv_b=lru_conv_b, lru_wr=lru_wr,
               lru_br=lru_br, lru_wi=lru_wi, lru_bi=lru_bi, lru_lambda=lru_lambda, hgrn_norm=hgrn_norm,
               rwkv_mu=rwkv_mu, rwkv_w0=rwkv_w0, rwkv_w2=rwkv_w2, rwkv_a0=rwkv_a0, rwkv_a2=rwkv_a2,
               rwkv_g2=rwkv_g2, rwkv_k_k=rwkv_k_k, rwkv_k_a=rwkv_k_a, rwkv_r_k=rwkv_r_k,
               rwkv_lnx_w=rwkv_lnx_w, rwkv_lnx_b=rwkv_lnx_b, w_out=w_out, ffn_norm=ffn_norm, ffn_up=ffn_up,
               ffn_conv_w=ffn_conv_w, ffn_conv_b=ffn_conv_b, ffn_down=ffn_down, ple_norm=ple_norm,
               ple_gate=ple_gate, ple_proj=ple_proj)
    lbs = hgrn_lower_bounds(hgrn_lb)
    layer_wts = [_layer_weights(prm, i) for i in range(DEPTH)]

    sample_states = (state_lru_h, state_lru_conv, state_hgrn, state_rwkv, state_rwkv_shift, state_ffn_conv)
    bp = x_prompt.shape[0]
    prompt_states = tuple(jnp.zeros((DEPTH, bp) + s.shape[2:], x_prompt.dtype) for s in sample_states)

    y_prompt, st_p = _trunk(x_prompt, p_prompt, prompt_states, prm, layer_wts, lbs, final_norm)
    y_sample, st_s = _trunk(x_sample, p_sample, sample_states, prm, layer_wts, lbs, final_norm)
    return (y_prompt, y_sample) + st_p + st_s
```

```python
import functools

import jax
import jax.numpy as jnp
import numpy as np
from jax import lax
from jax.experimental import pallas as pl
from jax.experimental.pallas import tpu as pltpu

F32 = jnp.float32
BF16 = jnp.bfloat16

D_MODEL = 4096
DEPTH = 4
W_LRU = 1024
LRU_BLOCKS = 8
LRU_BS = 128
LRU_CONV = 4
LRU_C = 8.0
W_HGRN = 1536
HGRN_KDIM = 128
HGRN_HEADS = 12
HGRN_VDIM = 128
HGRN_CHUNK = 16
W_RWKV = 1536
RWKV_HDIM = 64
RWKV_HEADS = 24
RWKV_PAIRS = RWKV_HEADS // 2
D_DECAY = 128
D_AAA = 128
D_GATE = 480
D_GATE_PAD = 512
RWKV_GN_EPS = 64e-5
RWKV_CHUNK = 64
N_RWKV_COLS = 3 * W_RWKV + D_DECAY + D_AAA + D_GATE
N_IN = 2 * W_LRU + 4 * W_HGRN + N_RWKV_COLS
FFN_DIM = 11008
FFN_CONV = 3
PLE_DIM = 256
EPS = 1e-6

LANES = 128
SUBLANES = 8
VMEM_LIMIT_BYTES = 56 * 1024 * 1024

ZC_OFF = 2 * W_LRU + 4 * W_HGRN
Z_WD = ZC_OFF
Z_AD = Z_WD + D_DECAY
Z_R = Z_AD + D_AAA
Z_K = Z_R + W_RWKV
Z_V = Z_K + W_RWKV
Z_GD = 13312
N_IN_PAD = Z_GD + D_GATE_PAD
ZC_REF_ORDER = ((Z_R, W_RWKV), (Z_WD, D_DECAY), (Z_K, W_RWKV), (Z_V, W_RWKV), (Z_AD, D_AAA), (Z_GD, D_GATE))


def _zc_to_kernel_layout(a, head=None):
    zeros = lambda w: jnp.zeros(a.shape[:-1] + (w,), a.dtype)
    placed, pos = {}, 0
    for off, width in ZC_REF_ORDER:
        placed[off] = a[..., pos:pos + width]
        pos += width
    parts, col = [zeros(ZC_OFF) if head is None else head], ZC_OFF
    for off in sorted(placed):
        if off > col:
            parts.append(zeros(off - col))
        parts.append(placed[off])
        col = off + placed[off].shape[-1]
    parts.append(zeros(N_IN_PAD - col))
    return jnp.concatenate(parts, axis=-1)


def _zc_from_kernel_layout(z):
    return jnp.concatenate([z[..., off:off + width] for off, width in ZC_REF_ORDER], axis=-1)


def _params(sem):
    return pltpu.CompilerParams(dimension_semantics=sem, vmem_limit_bytes=VMEM_LIMIT_BYTES)


def _sigmoid(x):
    return 1.0 / (1.0 + jnp.exp(-x))


def _gelu_tanh(x):
    return 0.5 * x * (1.0 + jnp.tanh(0.7978845608028654 * (x + 0.044715 * x * x * x)))


def _softplus(x):
    return jnp.maximum(x, 0.0) + jnp.log1p(jnp.exp(-jnp.abs(x)))


def _rmsnorm_kernel(x_ref, w_ref, o_ref):
    x = x_ref[...]
    ms = jnp.mean(x * x, axis=-1, keepdims=True)
    o_ref[...] = (x * lax.rsqrt(ms + EPS) * w_ref[...]).astype(o_ref.dtype)


def rmsnorm(x, w, out_dtype, tm=256):
    n, d = x.shape
    tm = min(tm, n)
    return pl.pallas_call(
        _rmsnorm_kernel,
        grid=(n // tm,),
        in_specs=[pl.BlockSpec((tm, d), lambda i: (i, 0)), pl.BlockSpec((1, d), lambda i: (0, 0))],
        out_specs=pl.BlockSpec((tm, d), lambda i: (i, 0)),
        out_shape=jax.ShapeDtypeStruct((n, d), out_dtype),
        compiler_params=_params(("parallel",)),
        name="rmsnorm",
    )(x, w.reshape(1, d))


def _mm_kernel(a_ref, w_ref, o_ref):
    o_ref[...] = jnp.dot(a_ref[...], w_ref[...], preferred_element_type=F32)


def _mm_res_kernel(a_ref, w_ref, r_ref, o_ref):
    o_ref[...] = r_ref[...] + jnp.dot(a_ref[...], w_ref[...], preferred_element_type=F32)


def matmul(a, w, layer, res=None, *, tm, tn, name):
    m, k = a.shape
    n = w.shape[2]
    tm = min(tm, m)
    in_specs = [pl.BlockSpec((tm, k), lambda i, j: (i, 0)),
                pl.BlockSpec((None, k, tn), lambda i, j: (layer, 0, j))]
    args = [a, w]
    kern = _mm_kernel
    if res is not None:
        in_specs.append(pl.BlockSpec((tm, tn), lambda i, j: (i, j)))
        args.append(res)
        kern = _mm_res_kernel
    return pl.pallas_call(
        kern,
        grid=(m // tm, n // tn),
        in_specs=in_specs,
        out_specs=pl.BlockSpec((tm, tn), lambda i, j: (i, j)),
        out_shape=jax.ShapeDtypeStruct((m, n), F32),
        compiler_params=_params(("parallel", "parallel")),
        name=name,
    )(*args)


def _outproj_kernel(a1_ref, a2_ref, a3_ref, w_ref, r_ref, o_ref):
    k1 = a1_ref.shape[1]
    k2 = k1 + a2_ref.shape[1]
    acc = jnp.dot(a1_ref[...], w_ref[:k1], preferred_element_type=F32)
    acc += jnp.dot(a2_ref[...], w_ref[k1:k2], preferred_element_type=F32)
    acc += jnp.dot(a3_ref[...], w_ref[k2:], preferred_element_type=F32)
    o_ref[...] = r_ref[...] + acc


def outproj(ya, yb, yc, w, layer, res, *, tm=1024, tn=512):
    m = ya.shape[0]
    k, n = w.shape[1:]
    tm = min(tm, m)
    a_specs = [pl.BlockSpec((tm, a.shape[1]), lambda i, j: (i, 0)) for a in (ya, yb, yc)]
    return pl.pallas_call(
        _outproj_kernel,
        grid=(m // tm, n // tn),
        in_specs=a_specs + [pl.BlockSpec((None, k, tn), lambda i, j: (layer, 0, j)),
                            pl.BlockSpec((tm, tn), lambda i, j: (i, j))],
        out_specs=pl.BlockSpec((tm, tn), lambda i, j: (i, j)),
        out_shape=jax.ShapeDtypeStruct((m, n), F32),
        compiler_params=_params(("parallel", "parallel")),
        name="outproj",
    )(ya, yb, yc, w, res)


def _ple_kernel(h_ref, wg_ref, p_ref, wp_ref, x_ref, o_ref):
    g = _sigmoid(jnp.dot(h_ref[...], wg_ref[...], preferred_element_type=F32))
    pp = jnp.dot(p_ref[...], wp_ref[...], preferred_element_type=F32)
    o_ref[...] = x_ref[...] + pp * g


def ple(h, wg, p, wp, layer, x, *, tm=1024, tn=512):
    m, d = x.shape
    tm = min(tm, m)
    return pl.pallas_call(
        _ple_kernel,
        grid=(m // tm, d // tn),
        in_specs=[pl.BlockSpec((tm, h.shape[1]), lambda i, j: (i, 0)),
                  pl.BlockSpec((None, wg.shape[1], tn), lambda i, j: (layer, 0, j)),
                  pl.BlockSpec((tm, p.shape[1]), lambda i, j: (i, 0)),
                  pl.BlockSpec((None, wp.shape[1], tn), lambda i, j: (layer, 0, j)),
                  pl.BlockSpec((tm, tn), lambda i, j: (i, j))],
        out_specs=pl.BlockSpec((tm, tn), lambda i, j: (i, j)),
        out_shape=jax.ShapeDtypeStruct((m, d), F32),
        compiler_params=_params(("parallel", "parallel")),
        name="ple",
    )(h, wg, p, wp, x)


def _lbs_kernel(lb_ref, o_ref):
    x = lb_ref[...]
    mx = jnp.max(x, axis=0, keepdims=True)
    e = jnp.exp(x - mx)
    sm = e / jnp.sum(e, axis=0, keepdims=True)
    acc = jnp.zeros_like(sm[0:1])
    for l in range(1, DEPTH):
        acc = acc + sm[l:l + 1]
        o_ref[l:l + 1, :] = acc
    o_ref[0:1, :] = jnp.zeros_like(acc)


def hgrn_lower_bounds(hgrn_lb):
    return pl.pallas_call(
        _lbs_kernel,
        out_shape=jax.ShapeDtypeStruct(hgrn_lb.shape, F32),
        name="hgrn_lbs",
    )(hgrn_lb)


def _shift_rows(x, hist, j, t8):
    r = x.shape[0]
    return jnp.where(t8 >= j, pltpu.roll(x, j, 0), pltpu.roll(hist, r - SUBLANES + j, 0))


def _hist_tile(x, hist8):
    r = x.shape[0]
    if r == SUBLANES:
        return hist8
    return jnp.concatenate([hist8, x[:r - SUBLANES]], axis=0)


def _lru_kernel(za_ref, zg_ref, hist_ref, h0_ref, cw_ref, cb_ref, wr_ref, br_ref, wi_ref, bi_ref,
                lam_ref, ya_ref, hout_ref, prev8_scr, hcar_scr, *, sb_n, r_n):
    i = pl.program_id(1)
    first = i == 0
    t8 = lax.broadcasted_iota(jnp.int32, (r_n, 1), 0) % SUBLANES
    lam = lam_ref[...]
    log_sig_lam = -_softplus(-lam)
    cw = cw_ref[...]
    for sb in range(sb_n):
        x = za_ref[sb]
        hist8 = jnp.where(first, hist_ref[sb], prev8_scr[...])
        hist = _hist_tile(x, hist8)
        xa = cb_ref[...] + cw[LRU_CONV - 1:LRU_CONV] * x
        for j in range(1, LRU_CONV):
            xa = xa + cw[LRU_CONV - 1 - j:LRU_CONV - j] * _shift_rows(x, hist, j, t8)
        prev8_scr[...] = x[r_n - SUBLANES:]
        r_lin, i_lin = [], []
        for n in range(LRU_BLOCKS):
            xb = xa[:, n * LRU_BS:(n + 1) * LRU_BS].astype(BF16)
            r_lin.append(jnp.dot(xb, wr_ref[n], preferred_element_type=F32))
            i_lin.append(jnp.dot(xb, wi_ref[n], preferred_element_type=F32))
        rg = _sigmoid(jnp.concatenate(r_lin, axis=1) + br_ref[...])
        ig = _sigmoid(jnp.concatenate(i_lin, axis=1) + bi_ref[...])
        log_a = LRU_C * rg * log_sig_lam
        a = jnp.exp(log_a)
        th = jnp.tanh(log_a)
        u = jnp.sqrt(-2.0 * th / (1.0 - th)) * ig * xa
        for s in (1, 2, 4):
            m = t8 >= s
            a_s = pltpu.roll(a, s, 0)
            u_s = pltpu.roll(u, s, 0)
            u = jnp.where(m, a * u_s + u, u)
            a = jnp.where(m, a * a_s, a)
        hprev = jnp.where(first, h0_ref[sb], hcar_scr[...])
        hs = []
        for g in range(r_n // SUBLANES):
            h = a[g * SUBLANES:(g + 1) * SUBLANES] * hprev + u[g * SUBLANES:(g + 1) * SUBLANES]
            hs.append(h)
            hprev = h[SUBLANES - 1:SUBLANES]
        hall = hs[0] if len(hs) == 1 else jnp.concatenate(hs, axis=0)
        hcar_scr[...] = hprev
        hout_ref[sb] = hprev
        ya_ref[sb] = (hall * _gelu_tanh(zg_ref[sb])).astype(ya_ref.dtype)


def lru_mixer(z3, hist, h0, prm, *, sb_n, r_n):
    nseq, seq, _ = z3.shape
    assert sb_n == 1 or r_n == seq
    nt = seq // r_n
    wblk = W_LRU // W_LRU
    row = lambda s, i: (s, i, 0)
    full2 = lambda s, i: (0, 0)
    full3 = lambda s, i: (0, 0, 0)
    kern = functools.partial(_lru_kernel, sb_n=sb_n, r_n=r_n)
    return pl.pallas_call(
        kern,
        grid=(nseq // sb_n, nt),
        in_specs=[
            pl.BlockSpec((sb_n, r_n, W_LRU), lambda s, i: (s, i, 0)),
            pl.BlockSpec((sb_n, r_n, W_LRU), lambda s, i: (s, i, wblk)),
            pl.BlockSpec((sb_n, SUBLANES, W_LRU), lambda s, i: (s, 0, 0)),
            pl.BlockSpec((sb_n, 1, W_LRU), lambda s, i: (s, 0, 0)),
            pl.BlockSpec((LRU_CONV, W_LRU), full2),
            pl.BlockSpec((1, W_LRU), full2),
            pl.BlockSpec((LRU_BLOCKS, LRU_BS, LRU_BS), full3),
            pl.BlockSpec((1, W_LRU), full2),
            pl.BlockSpec((LRU_BLOCKS, LRU_BS, LRU_BS), full3),
            pl.BlockSpec((1, W_LRU), full2),
            pl.BlockSpec((1, W_LRU), full2),
        ],
        out_specs=[pl.BlockSpec((sb_n, r_n, W_LRU), row),
                   pl.BlockSpec((sb_n, 1, W_LRU), lambda s, i: (s, 0, 0))],
        out_shape=[jax.ShapeDtypeStruct((nseq, seq, W_LRU), BF16),
                   jax.ShapeDtypeStruct((nseq, 1, W_LRU), F32)],
        scratch_shapes=[pltpu.VMEM((SUBLANES, W_LRU), F32), pltpu.VMEM((1, W_LRU), F32)],
        compiler_params=_params(("parallel", "arbitrary")),
        name="lru_mixer",
    )(z3, z3, hist, h0, prm['lru_conv_w'], prm['lru_conv_b'], prm['lru_wr'], prm['lru_br'],
      prm['lru_wi'], prm['lru_bi'], prm['lru_lambda'])


def _hgrn_kernel(zq_ref, zf_ref, zi_ref, zg_ref, s0_ref, lb_ref, nw_ref, yb_ref, sout_ref, st_scr,
                 *, sb_n, r_n, chunk):
    i = pl.program_id(2)
    first = i == 0
    nchunk = r_n // chunk
    tc = lax.broadcasted_iota(jnp.int32, (r_n, 1), 0) % chunk
    lb = lb_ref[...]
    scale = HGRN_KDIM ** -0.5
    for sb in range(sb_n):
        q = zq_ref[sb]
        q = q * _sigmoid(q)
        f = lb + (1.0 - lb) * _sigmoid(zf_ref[sb])
        k = 1.0 - f
        g = jnp.log(f)
        v = zi_ref[sb]
        b = g
        s = 1
        while s < chunk:
            b = jnp.where(tc >= s, b + pltpu.roll(b, s, 0), b)
            s *= 2
        b3 = b.reshape(nchunk, chunk, HGRN_KDIM)
        b_last = jnp.broadcast_to(b3[:, chunk - 1:chunk, :], b3.shape).reshape(r_n, HGRN_KDIM)
        qe = q * jnp.exp(b)
        kd = k * jnp.exp(b_last - b)
        o_in = jnp.zeros_like(q)
        for d in range(chunk):
            if d == 0:
                k_d, b_d, v_d = k, b, v
            else:
                k_d, b_d, v_d = pltpu.roll(k, d, 0), pltpu.roll(b, d, 0), pltpu.roll(v, d, 0)
            valid = tc >= d
            p = jnp.where(valid, q * k_d * jnp.exp(jnp.where(valid, b - b_d, 0.0)), 0.0)
            o_in = o_in + jnp.sum(p, axis=-1, keepdims=True) * v_d
        st = jnp.where(first, s0_ref[sb, 0].T, st_scr[...])
        outs = []
        for c in range(nchunk):
            lo, hi = c * chunk, (c + 1) * chunk
            inter = lax.dot_general(qe[lo:hi].astype(BF16), st.astype(BF16), (((1,), (1,)), ((), ())),
                                    preferred_element_type=F32)
            outs.append((inter + o_in[lo:hi]) * scale)
            upd = lax.dot_general(v[lo:hi].astype(BF16), kd[lo:hi].astype(BF16), (((0,), (0,)), ((), ())),
                                  preferred_element_type=F32)
            st = st * jnp.exp(b[hi - 1:hi]) + upd
        st_scr[...] = st
        sout_ref[sb, 0] = st.T
        o = outs[0] if nchunk == 1 else jnp.concatenate(outs, axis=0)
        o = o * lax.rsqrt(jnp.mean(o * o, axis=-1, keepdims=True) + EPS) * nw_ref[...]
        zg = zg_ref[sb]
        yb_ref[sb] = (o * (zg * _sigmoid(zg))).astype(yb_ref.dtype)


def hgrn_mixer(z3, s0, layer, lb_row, norm_w, *, sb_n, r_n):
    nseq, seq, _ = z3.shape
    assert sb_n == 1 or r_n == seq
    nt = seq // r_n
    chunk = min(HGRN_CHUNK, r_n)
    c0 = 2 * W_LRU // HGRN_KDIM

    def zspec(k):
        return pl.BlockSpec((sb_n, r_n, HGRN_KDIM), lambda s, h, i, k=k: (s, i, c0 + k * HGRN_HEADS + h))

    kern = functools.partial(_hgrn_kernel, sb_n=sb_n, r_n=r_n, chunk=chunk)
    return pl.pallas_call(
        kern,
        grid=(nseq // sb_n, HGRN_HEADS, nt),
        in_specs=[zspec(0), zspec(1), zspec(2), zspec(3),
                  pl.BlockSpec((None, sb_n, 1, HGRN_KDIM, HGRN_VDIM), lambda s, h, i: (layer, s, h, 0, 0)),
                  pl.BlockSpec((1, HGRN_KDIM), lambda s, h, i: (0, h)),
                  pl.BlockSpec((1, HGRN_VDIM), lambda s, h, i: (0, 0))],
        out_specs=[pl.BlockSpec((sb_n, r_n, HGRN_VDIM), lambda s, h, i: (s, i, h)),
                   pl.BlockSpec((sb_n, 1, HGRN_KDIM, HGRN_VDIM), lambda s, h, i: (s, h, 0, 0))],
        out_shape=[jax.ShapeDtypeStruct((nseq, seq, W_HGRN), BF16),
                   jax.ShapeDtypeStruct((nseq, HGRN_HEADS, HGRN_KDIM, HGRN_VDIM), F32)],
        scratch_shapes=[pltpu.VMEM((HGRN_VDIM, HGRN_KDIM), F32)],
        compiler_params=_params(("parallel", "parallel", "arbitrary")),
        name="hgrn_mixer",
    )(z3, z3, z3, z3, s0, lb_row, norm_w)


def _seg_sum(x, lane_lo):
    s0 = jnp.sum(jnp.where(lane_lo, x, 0.0), axis=-1, keepdims=True)
    s1 = jnp.sum(jnp.where(lane_lo, 0.0, x), axis=-1, keepdims=True)
    return jnp.where(lane_lo, s0, s1)


def _rwkv_kernel(zr_ref, zwd_ref, zk_ref, zv_ref, zad_ref, zgd_ref,
                 shr_ref, shwd_ref, shk_ref, shv_ref, shad_ref, shgd_ref,
                 mur_ref, muwd_ref, muk_ref, muv_ref, muad_ref, mugd_ref,
                 w0_ref, w2_ref, a0_ref, a2_ref, g2_ref, kk_ref, ka_ref, rk_ref, lw_ref, lb_ref,
                 s0_ref, yc_ref, sout_ref,
                 car_r, car_wd, car_k, car_v, car_ad, car_gd, s_scr, *, sb_n, r_n, pg_n, chunk):
    i = pl.program_id(2)
    first = i == 0
    nchunk = r_n // chunk
    row0 = lax.broadcasted_iota(jnp.int32, (r_n, 1), 0) == 0
    tc = lax.broadcasted_iota(jnp.int32, (r_n, 1), 0) % chunk
    lane = lax.broadcasted_iota(jnp.int32, (1, LANES), 1)
    lane_lo = lane < RWKV_HDIM

    def mix(z_ref, sh_ref, mu_ref, car, sb):
        x = z_ref[sb]
        prev0 = jnp.where(first, sh_ref[sb], car[sb])
        prev = jnp.where(row0, prev0, pltpu.roll(x, 1, 0))
        car[sb] = x[r_n - 1:r_n]
        return x + (prev - x) * mu_ref[...]

    wp = pg_n * LANES
    seqs = []
    for sb in range(sb_n):
        wdm = mix(zwd_ref, shwd_ref, muwd_ref, car_wd, sb)
        adm = mix(zad_ref, shad_ref, muad_ref, car_ad, sb)
        gdm = mix(zgd_ref, shgd_ref, mugd_ref, car_gd, sb)
        rm = mix(zr_ref, shr_ref, mur_ref, car_r, sb)
        km = mix(zk_ref, shk_ref, muk_ref, car_k, sb)
        vm = mix(zv_ref, shv_ref, muv_ref, car_v, sb)
        th = jnp.tanh(wdm).astype(BF16)
        sg = _sigmoid(gdm).astype(BF16)
        wl = w0_ref[...] + jnp.dot(th, w2_ref[...], preferred_element_type=F32)
        w_log = -_softplus(-wl) - 0.5
        lw = -jnp.exp(w_log)
        aa = _sigmoid(a0_ref[...] + jnp.dot(adm.astype(BF16), a2_ref[...], preferred_element_type=F32))
        gate = jnp.dot(sg, g2_ref[...], preferred_element_type=F32)
        cw = lw
        s = 1
        while s < chunk:
            cw = jnp.where(tc >= s, cw + pltpu.roll(cw, s, 0), cw)
            s *= 2
        cw3 = cw.reshape(nchunk, chunk, wp)
        cw_last = jnp.broadcast_to(cw3[:, chunk - 1:chunk, :], cw3.shape).reshape(r_n, wp)
        to_end = jnp.exp(cw_last - cw)
        w_fwd = jnp.exp(cw)
        w_inv = jnp.exp(-cw)
        w_prev = jnp.exp(cw - lw)
        kmods, ops = [], []
        for pg in range(pg_n):
            cs = slice(pg * LANES, (pg + 1) * LANES)
            kr = km[:, cs]
            kkr = kr * kk_ref[:, cs]
            nrm = jnp.sqrt(_seg_sum(kkr * kkr, lane_lo))
            kk = kkr / jnp.maximum(nrm, 1e-12)
            a_p = aa[:, cs]
            kmod = kr * (1.0 + (a_p - 1.0) * ka_ref[:, cs])
            kmods.append(kmod)
            beta = kk * a_p
            ops.append(dict(
                a_t=-kk * w_prev[:, cs],
                b_t=beta * w_inv[:, cs],
                k_t=kmod * w_inv[:, cs],
                r_t=rm[:, cs] * w_fwd[:, cs],
                v=vm[:, cs],
                b_end=beta * to_end[:, cs],
                k_end=kmod * to_end[:, cs],
                w_end=jnp.exp(cw_last[:, cs])))
        kmod_w = kmods[0] if pg_n == 1 else jnp.concatenate(kmods, axis=1)
        seqs.append(dict(ops=ops, rm=rm, vm=vm, gate=gate, kmod=kmod_w))

    states = {}
    for sb in range(sb_n):
        for pg in range(pg_n):
            states[sb, pg] = jnp.where(first, _pair_to_blockdiag(s0_ref[sb, pg], lane_lo), s_scr[sb, pg])

    names = ('a_t', 'b_t', 'k_t', 'r_t', 'v', 'b_end', 'k_end')
    y_out = {}
    if chunk == RWKV_CHUNK:
        for c in range(nchunk):
            rs = slice(c * chunk, (c + 1) * chunk)
            units = []
            for sb in range(sb_n):
                for pg in range(pg_n):
                    op = seqs[sb]['ops'][pg]
                    u = {n: op[n][rs] for n in names}
                    u['w_end'] = [op['w_end'][c * chunk:c * chunk + 1]]
                    u['keys'] = [(sb, pg)]
                    units.append(u)
            ys = _rwkv_units(units, states, lane_lo, chunk)
            for u, y in zip(units, ys):
                y_out.setdefault(u['keys'][0], []).append(y)
    else:
        group = RWKV_CHUNK // chunk
        assert nchunk == 1 and sb_n % group == 0
        units = []
        for sg in range(sb_n // group):
            for pg in range(pg_n):
                members = [seqs[sg * group + g]['ops'][pg] for g in range(group)]
                u = {n: jnp.concatenate([m[n] for m in members], axis=0) for n in names}
                u['w_end'] = [m['w_end'][0:1] for m in members]
                u['keys'] = [(sg * group + g, pg) for g in range(group)]
                units.append(u)
        ys = _rwkv_units(units, states, lane_lo, chunk)
        for u, y in zip(units, ys):
            for g, key in enumerate(u['keys']):
                y_out[key] = [y[g * chunk:(g + 1) * chunk]]

    for sb in range(sb_n):
        for pg in range(pg_n):
            st = states[sb, pg]
            s_scr[sb, pg] = st
            sout_ref[sb, pg] = st[:RWKV_HDIM] + st[RWKV_HDIM:]
        cols = [ys[0] if len(ys) == 1 else jnp.concatenate(ys, axis=0) for ys in (y_out[sb, pg] for pg in range(pg_n))]
        y = cols[0] if pg_n == 1 else jnp.concatenate(cols, axis=1)
        sq = seqs[sb]
        mu = _seg_sum_wide(y, pg_n) * (1.0 / RWKV_HDIM)
        yc = y - mu
        var = _seg_sum_wide(yc * yc, pg_n) * (1.0 / RWKV_HDIM)
        yn = yc * lax.rsqrt(var + RWKV_GN_EPS) * lw_ref[...] + lb_ref[...]
        bonus = _seg_sum_wide(sq['rm'] * sq['kmod'] * rk_ref[...], pg_n) * sq['vm']
        yc_ref[sb] = ((yn + bonus) * sq['gate']).astype(yc_ref.dtype)


def _pair_to_blockdiag(s_pair, lane_lo):
    return jnp.concatenate([jnp.where(lane_lo, s_pair, 0.0), jnp.where(lane_lo, 0.0, s_pair)], axis=0)


def _split_bf16(x):
    hi = x.astype(BF16)
    lo = (x - hi.astype(F32)).astype(BF16)
    return hi, lo


def _dot3(a, b, ca, cb):
    a3 = jnp.concatenate([a[0], a[0], a[1]], axis=ca)
    b3 = jnp.concatenate([b[0], b[1], b[0]], axis=cb)
    return lax.dot_general(a3, b3, (((ca,), (cb,)), ((), ())), preferred_element_type=F32)


def _stack_heads(x, lane_lo):
    return jnp.concatenate([jnp.where(lane_lo, x, 0.0), jnp.where(lane_lo, 0.0, x)], axis=0)


def _rwkv_units(units, states, lane_lo, seq_rows):
    n = RWKV_CHUNK
    rows = lax.broadcasted_iota(jnp.int32, (n, LANES), 0)
    s_idx = lax.broadcasted_iota(jnp.int32, (n, LANES), 1) % RWKV_HDIM
    same_seq = (rows // seq_rows) == (s_idx // seq_rows)
    strict = jnp.logical_and(same_seq, s_idx < rows)
    incl = jnp.logical_and(same_seq, s_idx <= rows)
    r2 = lax.broadcasted_iota(jnp.int32, (LANES, LANES), 0)
    c2 = lax.broadcasted_iota(jnp.int32, (LANES, LANES), 1)
    eye = jnp.where(r2 == c2, 1.0, 0.0)
    same_head = (r2 < RWKV_HDIM) == (c2 < RWKV_HDIM)

    for u in units:
        u['ar'] = _split_bf16(jnp.concatenate([u['a_t'], u['r_t']], axis=0))
        pb = _dot3(u['ar'], _split_bf16(_stack_heads(u['b_t'], lane_lo)), 1, 1)
        pk = _dot3(u['ar'], _split_bf16(_stack_heads(u['k_t'], lane_lo)), 1, 1)
        u['n_b'] = jnp.where(incl, pb[n:], 0.0)
        u['mn_k'] = jnp.concatenate([jnp.where(strict, pk[:n], 0.0), jnp.where(incl, pk[n:], 0.0)], axis=0)
        x = _stack_heads(jnp.where(strict, pb[:n], 0.0), lane_lo)
        u['t_acc'] = eye + x
        u['xs'] = _split_bf16(x)
    span = 2
    while span < seq_rows:
        for u in units:
            u['xs'] = _split_bf16(_dot3(u['xs'], u['xs'], 1, 0))
        for u in units:
            u['t_acc'] = u['t_acc'] + _dot3(_split_bf16(u['t_acc']), u['xs'], 1, 0)
        span *= 2
    for u in units:
        u['t_p'] = _split_bf16(u['t_acc'][:n] + u['t_acc'][n:])
        u['zy'] = _dot3(_split_bf16(u['mn_k']), _split_bf16(_stack_heads(u['v'], lane_lo)), 1, 0)
    for u in units:
        if len(u['keys']) == 1:
            ar0 = _dot3(u['ar'], _split_bf16(states[u['keys'][0]]), 1, 1)
            u['a0'], u['r0'] = ar0[:n], ar0[n:]
        else:
            a0, r0 = [], []
            for g, key in enumerate(u['keys']):
                gs = slice(g * seq_rows, (g + 1) * seq_rows)
                ar_g = _split_bf16(jnp.concatenate([u['a_t'][gs], u['r_t'][gs]], axis=0))
                o = _dot3(ar_g, _split_bf16(states[key]), 1, 1)
                a0.append(o[:seq_rows])
                r0.append(o[seq_rows:])
            u['a0'], u['r0'] = jnp.concatenate(a0, axis=0), jnp.concatenate(r0, axis=0)
    for u in units:
        u['u'] = _dot3(u['t_p'], _split_bf16(_stack_heads(u['a0'] + u['zy'][:n], lane_lo)), 1, 0)
    ys = []
    for u in units:
        ys.append(u['r0'] + u['zy'][n:]
                  + _dot3(_split_bf16(u['n_b']), _split_bf16(_stack_heads(u['u'], lane_lo)), 1, 0))
        for g, key in enumerate(u['keys']):
            gs = slice(g * seq_rows, (g + 1) * seq_rows)
            upd = _dot3(_split_bf16(jnp.concatenate([u['u'][gs], u['v'][gs]], axis=0)),
                        _split_bf16(jnp.concatenate([u['b_end'][gs], u['k_end'][gs]], axis=0)), 0, 0)
            states[key] = states[key] * u['w_end'][g] + jnp.where(same_head, upd, 0.0)
    return ys


def _seg_sum_wide(x, pg_n):
    lane = lax.broadcasted_iota(jnp.int32, (1, LANES), 1)
    lane_lo = lane < RWKV_HDIM
    parts = [_seg_sum(x[:, pg * LANES:(pg + 1) * LANES], lane_lo) for pg in range(pg_n)]
    return parts[0] if pg_n == 1 else jnp.concatenate(parts, axis=1)


def rwkv_mixer(z3, shift, s0, layer, prm, *, sb_n, r_n, pg_n):
    nseq, seq, _ = z3.shape
    nt = seq // r_n
    wp = pg_n * LANES
    npg = RWKV_PAIRS // pg_n

    def zspec(off, width, per_pair):
        base = off // width
        if per_pair:
            return pl.BlockSpec((sb_n, r_n, width), lambda s, p, i: (s, i, base + p))
        return pl.BlockSpec((sb_n, r_n, width), lambda s, p, i: (s, i, base))

    def shspec(off, width, per_pair):
        base = off // width
        if per_pair:
            return pl.BlockSpec((sb_n, 1, width), lambda s, p, i: (s, 0, base + p))
        return pl.BlockSpec((sb_n, 1, width), lambda s, p, i: (s, 0, base))

    def muspec(off, width, per_pair):
        base = off // width
        if per_pair:
            return pl.BlockSpec((1, width), lambda s, p, i: (0, base + p))
        return pl.BlockSpec((1, width), lambda s, p, i: (0, base))

    def pspec(rows):
        return pl.BlockSpec((rows, wp), lambda s, p, i: (0, p))

    pieces = [(Z_R, wp, True), (Z_WD, D_DECAY, False), (Z_K, wp, True), (Z_V, wp, True),
              (Z_AD, D_AAA, False), (Z_GD, D_GATE_PAD, False)]
    for off, width, _ in pieces:
        assert off % width == 0, (off, width)
    in_specs = ([zspec(*p) for p in pieces] + [shspec(*p) for p in pieces] + [muspec(*p) for p in pieces]
                + [pspec(1), pspec(D_DECAY), pspec(1), pspec(D_AAA), pspec(D_GATE_PAD),
                   pspec(1), pspec(1), pspec(1), pspec(1), pspec(1),
                   pl.BlockSpec((None, sb_n, pg_n, RWKV_HDIM, LANES), lambda s, p, i: (layer, s, p, 0, 0))])
    chunk = min(RWKV_CHUNK, r_n)
    kern = functools.partial(_rwkv_kernel, sb_n=sb_n, r_n=r_n, pg_n=pg_n, chunk=chunk)
    scratch = ([pltpu.VMEM((sb_n, 1, w), F32) for w in (wp, D_DECAY, wp, wp, D_AAA, D_GATE_PAD)]
               + [pltpu.VMEM((sb_n, pg_n, LANES, LANES), F32)])
    mu = prm['rwkv_mu']
    return pl.pallas_call(
        kern,
        grid=(nseq // sb_n, npg, nt),
        in_specs=in_specs,
        out_specs=[pl.BlockSpec((sb_n, r_n, wp), lambda s, p, i: (s, i, p)),
                   pl.BlockSpec((sb_n, pg_n, RWKV_HDIM, LANES), lambda s, p, i: (s, p, 0, 0))],
        out_shape=[jax.ShapeDtypeStruct((nseq, seq, W_RWKV), BF16),
                   jax.ShapeDtypeStruct((nseq, RWKV_PAIRS, RWKV_HDIM, LANES), F32)],
        scratch_shapes=scratch,
        compiler_params=_params(("parallel", "parallel", "arbitrary")),
        name="rwkv_mixer",
    )(z3, z3, z3, z3, z3, z3, shift, shift, shift, shift, shift, shift, mu, mu, mu, mu, mu, mu,
      prm['rwkv_w0'], prm['rwkv_w2'], prm['rwkv_a0'], prm['rwkv_a2'], prm['rwkv_g2'],
      prm['rwkv_k_k'], prm['rwkv_k_a'], prm['rwkv_r_k'], prm['rwkv_lnx_w'], prm['rwkv_lnx_b'], s0)


FFN_COL_CHUNK = 256
FFN_UNROLL = 2


def _ffn_act_kernel(ug_ref, uv_ref, hg_ref, hv_ref, cwg_ref, cwv_ref, cbg_ref, cbv_ref, o_ref,
                    pg_scr, pv_scr, *, sb_n, r_n):
    i = pl.program_id(1)
    first = i == 0
    rows = sb_n * r_n
    t8 = lax.broadcasted_iota(jnp.int32, (rows, 1), 0) % SUBLANES

    def conv(x, hist, cw, cb):
        y = cb + cw[FFN_CONV - 1:FFN_CONV] * x
        for j in range(1, FFN_CONV):
            y = y + cw[FFN_CONV - 1 - j:FFN_CONV - j] * _shift_rows(x, hist, j, t8)
        return y

    def body(c, carry):
        cs = pl.ds(pl.multiple_of(c * FFN_COL_CHUNK, FFN_COL_CHUNK), FFN_COL_CHUNK)
        if sb_n == 1:
            xg, xv = ug_ref[0, :, cs], uv_ref[0, :, cs]
            hg = _hist_tile(xg, jnp.where(first, hg_ref[0, :, cs], pg_scr[:, cs]))
            hv = _hist_tile(xv, jnp.where(first, hv_ref[0, :, cs], pv_scr[:, cs]))
            pg_scr[:, cs] = xg[r_n - SUBLANES:]
            pv_scr[:, cs] = xv[r_n - SUBLANES:]
        else:
            flat = lambda ref: ref[:, :, cs].reshape(rows, FFN_COL_CHUNK)
            xg, xv, hg, hv = flat(ug_ref), flat(uv_ref), flat(hg_ref), flat(hv_ref)
        g = conv(xg, hg, cwg_ref[:, cs], cbg_ref[:, cs])
        v = conv(xv, hv, cwv_ref[:, cs], cbv_ref[:, cs])
        o_ref[:, :, cs] = (_gelu_tanh(g) * v).astype(o_ref.dtype).reshape(sb_n, r_n, FFN_COL_CHUNK)
        return carry

    lax.fori_loop(0, FFN_DIM // FFN_COL_CHUNK, body, 0, unroll=FFN_UNROLL)


def ffn_act(u3, hist, conv_w, conv_b, *, sb_n, r_n):
    nseq, seq, _ = u3.shape
    assert sb_n == 1 or r_n == seq == SUBLANES
    nt = seq // r_n
    kern = functools.partial(_ffn_act_kernel, sb_n=sb_n, r_n=r_n)
    return pl.pallas_call(
        kern,
        grid=(nseq // sb_n, nt),
        in_specs=[pl.BlockSpec((sb_n, r_n, FFN_DIM), lambda s, i: (s, i, 0)),
                  pl.BlockSpec((sb_n, r_n, FFN_DIM), lambda s, i: (s, i, 1)),
                  pl.BlockSpec((sb_n, SUBLANES, FFN_DIM), lambda s, i: (s, 0, 0)),
                  pl.BlockSpec((sb_n, SUBLANES, FFN_DIM), lambda s, i: (s, 0, 1)),
                  pl.BlockSpec((FFN_CONV, FFN_DIM), lambda s, i: (0, 0)),
                  pl.BlockSpec((FFN_CONV, FFN_DIM), lambda s, i: (0, 1)),
                  pl.BlockSpec((1, FFN_DIM), lambda s, i: (0, 0)),
                  pl.BlockSpec((1, FFN_DIM), lambda s, i: (0, 1))],
        out_specs=pl.BlockSpec((sb_n, r_n, FFN_DIM), lambda s, i: (s, i, 0)),
        out_shape=jax.ShapeDtypeStruct((nseq, seq, FFN_DIM), BF16),
        scratch_shapes=[pltpu.VMEM((SUBLANES, FFN_DIM), F32), pltpu.VMEM((SUBLANES, FFN_DIM), F32)],
        compiler_params=_params(("parallel", "arbitrary")),
        name="ffn_act",
    )(u3, u3, hist, hist, conv_w, conv_w, conv_b, conv_b)


def _hist_from_state(buf):
    nseq, kb, c = buf.shape
    return jnp.concatenate([jnp.zeros((nseq, SUBLANES - kb, c), buf.dtype), buf], axis=1)


def _tiles(seq):
    if seq > 256:
        return dict(lru=(1, 256), hgrn=(1, 256), ffn=(1, 128))
    return dict(lru=(8, seq), hgrn=(8, seq), ffn=(16, seq))


def _rwkv_tiles(nseq, seq):
    if seq > 256:
        return dict(sb_n=nseq, r_n=128, pg_n=3)
    return dict(sb_n=8, r_n=seq, pg_n=3)


def _layer(x, p_bf, st, prm, wts, big, layer, lb_row):
    lru_h, lru_conv, hgrn_all, rwkv_all, rwkv_shift, ffn_conv = st
    nseq, seq, d = x.shape
    n = nseq * seq
    x2 = x.reshape(n, d)
    tl = _tiles(seq)

    h = rmsnorm(x2, prm['attn_norm'], BF16)
    z = matmul(h, big['w_in'], layer, tm=1024, tn=512, name="in_proj")
    z3 = z.reshape(nseq, seq, N_IN_PAD)

    sb_n, r_n = tl['lru']
    ya, new_lru_h = lru_mixer(z3, _hist_from_state(lru_conv), lru_h.reshape(nseq, 1, W_LRU), wts,
                              sb_n=sb_n, r_n=r_n)
    new_lru_conv = z3[:, seq - (LRU_CONV - 1):, :W_LRU]

    sb_n, r_n = tl['hgrn']
    yb, new_hgrn = hgrn_mixer(z3, hgrn_all, layer, lb_row, wts['hgrn_norm'], sb_n=sb_n, r_n=r_n)

    shift = _zc_to_kernel_layout(rwkv_shift).reshape(nseq, 1, N_IN_PAD)
    yc, new_rwkv = rwkv_mixer(z3, shift, rwkv_all, layer, wts, **_rwkv_tiles(nseq, seq))
    new_shift = _zc_from_kernel_layout(z3[:, seq - 1, :])

    x2 = outproj(ya.reshape(n, W_LRU), yb.reshape(n, W_HGRN), yc.reshape(n, W_RWKV), big['w_out'], layer, x2)

    h2 = rmsnorm(x2, prm['ffn_norm'], BF16)
    u = matmul(h2, big['ffn_up'], layer, tm=1024, tn=512, name="ffn_up")
    u3 = u.reshape(nseq, seq, 2 * FFN_DIM)
    sb_n, r_n = tl['ffn']
    act = ffn_act(u3, _hist_from_state(ffn_conv), wts['ffn_conv_w'], wts['ffn_conv_b'], sb_n=sb_n, r_n=r_n)
    new_ffn_conv = u3[:, seq - (FFN_CONV - 1):, :]
    x2 = matmul(act.reshape(n, FFN_DIM), big['ffn_down'], layer, x2, tm=512, tn=256, name="ffn_down")

    h3 = rmsnorm(x2, prm['ple_norm'], BF16)
    x2 = ple(h3, big['ple_gate'], p_bf.reshape(n, PLE_DIM), big['ple_proj'], layer, x2)

    new = (new_lru_h.reshape(nseq, W_LRU), new_lru_conv, new_hgrn, new_rwkv, new_shift, new_ffn_conv)
    return x2.reshape(nseq, seq, d), new


def _layer_weights(prm, i):
    row = lambda a: a[i].reshape(1, -1)
    mu = _zc_to_kernel_layout(prm['rwkv_mu'][i]).reshape(1, N_IN_PAD)
    return dict(
        lru_conv_w=prm['lru_conv_w'][i], lru_conv_b=row(prm['lru_conv_b']),
        lru_wr=prm['lru_wr'][i].astype(BF16), lru_br=row(prm['lru_br']),
        lru_wi=prm['lru_wi'][i].astype(BF16), lru_bi=row(prm['lru_bi']),
        lru_lambda=row(prm['lru_lambda']),
        hgrn_norm=row(prm['hgrn_norm']),
        rwkv_mu=mu, rwkv_w0=row(prm['rwkv_w0']), rwkv_w2=prm['rwkv_w2'][i].astype(BF16),
        rwkv_a0=row(prm['rwkv_a0']), rwkv_a2=prm['rwkv_a2'][i].astype(BF16),
        rwkv_g2=jnp.pad(prm['rwkv_g2'][i].astype(BF16), ((0, D_GATE_PAD - D_GATE), (0, 0))),
        rwkv_k_k=row(prm['rwkv_k_k']), rwkv_k_a=row(prm['rwkv_k_a']), rwkv_r_k=row(prm['rwkv_r_k']),
        rwkv_lnx_w=row(prm['rwkv_lnx_w']), rwkv_lnx_b=row(prm['rwkv_lnx_b']),
        ffn_conv_w=prm['ffn_conv_w'][i], ffn_conv_b=row(prm['ffn_conv_b']),
    )


def _big_weights(prm):
    w_in_bf = prm['w_in'].astype(BF16)
    return dict(
        w_in=_zc_to_kernel_layout(w_in_bf[..., ZC_OFF:], head=w_in_bf[..., :ZC_OFF]),
        w_out=prm['w_out'].astype(BF16), ffn_up=prm['ffn_up'].astype(BF16),
        ffn_down=prm['ffn_down'].astype(BF16), ple_gate=prm['ple_gate'].astype(BF16),
        ple_proj=prm['ple_proj'].astype(BF16))


def _pack_rwkv_state(s):
    lead = s.shape[:-3]
    s = s.reshape(lead + (RWKV_PAIRS, 2, RWKV_HDIM, RWKV_HDIM))
    s = jnp.swapaxes(s, -3, -2)
    return s.reshape(lead + (RWKV_PAIRS, RWKV_HDIM, LANES))


def _unpack_rwkv_state(s):
    lead = s.shape[:-3]
    s = s.reshape(lead + (RWKV_PAIRS, RWKV_HDIM, 2, RWKV_HDIM))
    s = jnp.swapaxes(s, -3, -2)
    return s.reshape(lead + (RWKV_HEADS, RWKV_HDIM, RWKV_HDIM))


def _trunk(x, p, states, prm, layer_wts, big, lbs, final_norm):
    nseq, seq, d = x.shape
    p_bf = p.astype(BF16)
    lru_h, lru_conv, hgrn_s, rwkv_s, rwkv_shift, ffn_conv = states
    rwkv_packed = _pack_rwkv_state(rwkv_s)
    outs = [[] for _ in states]
    for i in range(DEPTH):
        norms = {k: prm[k][i] for k in ('attn_norm', 'ffn_norm', 'ple_norm')}
        st = (lru_h[i], lru_conv[i], hgrn_s, rwkv_packed, rwkv_shift[i], ffn_conv[i])
        x, ns = _layer(x, p_bf[i], st, norms, layer_wts[i], big, i, lbs[i:i + 1])
        for lst, nst in zip(outs, ns):
            lst.append(nst)
    y = rmsnorm(x.reshape(nseq * seq, d), final_norm, F32).reshape(nseq, seq, d)
    new = [jnp.stack(lst) for lst in outs]
    new[3] = _unpack_rwkv_state(new[3])
    return y, tuple(new)


def kernel(x_prompt, x_sample, state_lru_h, state_lru_conv, state_hgrn, state_rwkv, state_rwkv_shift,
           state_ffn_conv, p_prompt, p_sample, attn_norm, w_in, lru_conv_w, lru_conv_b, lru_wr, lru_br,
           lru_wi, lru_bi, lru_lambda, hgrn_lb, hgrn_norm, rwkv_mu, rwkv_w0, rwkv_w2, rwkv_a0, rwkv_a2,
           rwkv_g2, rwkv_k_k, rwkv_k_a, rwkv_r_k, rwkv_lnx_w, rwkv_lnx_b, w_out, ffn_norm, ffn_up,
           ffn_conv_w, ffn_conv_b, ffn_down, ple_norm, ple_gate, ple_proj, final_norm):
    prm = dict(attn_norm=attn_norm, w_in=w_in, lru_conv_w=lru_conv_w, lru_conv_b=lru_conv_b, lru_wr=lru_wr,
               lru_br=lru_br, lru_wi=lru_wi, lru_bi=lru_bi, lru_lambda=lru_lambda, hgrn_norm=hgrn_norm,
               rwkv_mu=rwkv_mu, rwkv_w0=rwkv_w0, rwkv_w2=rwkv_w2, rwkv_a0=rwkv_a0, rwkv_a2=rwkv_a2,
               rwkv_g2=rwkv_g2, rwkv_k_k=rwkv_k_k, rwkv_k_a=rwkv_k_a, rwkv_r_k=rwkv_r_k,
               rwkv_lnx_w=rwkv_lnx_w, rwkv_lnx_b=rwkv_lnx_b, w_out=w_out, ffn_norm=ffn_norm, ffn_up=ffn_up,
               ffn_conv_w=ffn_conv_w, ffn_conv_b=ffn_conv_b, ffn_down=ffn_down, ple_norm=ple_norm,
               ple_gate=ple_gate, ple_proj=ple_proj)
    lbs = hgrn_lower_bounds(hgrn_lb)
    layer_wts = [_layer_weights(prm, i) for i in range(DEPTH)]
    big = _big_weights(prm)

    sample_states = (state_lru_h, state_lru_conv, state_hgrn, state_rwkv, state_rwkv_shift, state_ffn_conv)
    bp = x_prompt.shape[0]
    prompt_states = tuple(jnp.zeros((DEPTH, bp) + s.shape[2:], x_prompt.dtype) for s in sample_states)

    y_prompt, st_p = _trunk(x_prompt, p_prompt, prompt_states, prm, layer_wts, big, lbs, final_norm)
    y_sample, st_s = _trunk(x_sample, p_sample, sample_states, prm, layer_wts, big, lbs, final_norm)
    return (y_prompt, y_sample) + st_p + st_s
```
